```python
import math
import jax, jax.numpy as jnp
from jax import lax
import numpy as np

D_MODEL = 1024
BATCH = 16
SEQ = 2048
DEPTH = 1

N_DIFF_HEADS = 8
DIFF_HEAD_DIM = 64
ATTN_WIDTH = N_DIFF_HEADS * 2 * DIFF_HEAD_DIM
Q_BLOCK = 128
SGU_WIDTH = 1024
N_SGU_GROUPS = 8
SGU_GROUP_DIM = SGU_WIDTH // N_SGU_GROUPS
SGU_CHUNK = 128
N_GROUPS = 4
EXPERTS_PER_GROUP = 8
N_EXPERTS = N_GROUPS * EXPERTS_PER_GROUP
TOP_K_IN_GROUP = 2
D_EXPERT = 256
IN_SIZES = [ATTN_WIDTH, ATTN_WIDTH, ATTN_WIDTH, SGU_WIDTH, SGU_WIDTH, D_MODEL, D_MODEL]
IN_WIDTH = sum(IN_SIZES)
IN_SPLITS = [int(v) for v in np.cumsum(IN_SIZES)[:-1]]
EPS = 1e-6
ALIBI_SLOPES = np.array([2.0 ** (-8.0 * (h + 1) / N_DIFF_HEADS) for h in range(N_DIFF_HEADS)], dtype=np.float32)

kernel_name = "hybrid_diffattn_sgu_hmoe_encoder"


def lambda_init_fn(layer):
    return 0.8 - 0.6 * math.exp(-0.3 * layer)


def rmsnorm(x, g):
    xf = x.astype(jnp.float32)
    y = xf * lax.rsqrt(jnp.mean(xf * xf, axis=-1, keepdims=True) + EPS)
    return y.astype(x.dtype) * g


def layernorm(x, g, b):
    xf = x.astype(jnp.float32)
    mu = jnp.mean(xf, axis=-1, keepdims=True)
    var = jnp.mean(jnp.square(xf - mu), axis=-1, keepdims=True)
    return ((xf - mu) * lax.rsqrt(var + EPS)).astype(x.dtype) * g + b


def modulate(h, shift, scale):
    return h * (1.0 + scale[:, None, :]) + shift[:, None, :]


def diff_attention(q, k, v, lam, sub_g, lam_init):
    B, S = q.shape[0], q.shape[1]
    scale = DIFF_HEAD_DIM ** -0.5
    slopes = jnp.asarray(ALIBI_SLOPES)
    kpos = jnp.arange(S)

    def block(i):
        start = i * Q_BLOCK
        qb = lax.dynamic_slice_in_dim(q, start, Q_BLOCK, axis=1)
        qpos = start + jnp.arange(Q_BLOCK)
        dist = jnp.abs(qpos[:, None] - kpos[None, :]).astype(jnp.float32)
        bias = -slopes[:, None, None, None] * dist[None, None]
        s = jnp.einsum('bqhmd,bkhmd->bhmqk', qb, k).astype(jnp.float32) * scale + bias
        p = jax.nn.softmax(s, axis=-1)
        a = p[:, :, 0] - lam * p[:, :, 1]
        return jnp.einsum('bhqk,bkhe->bqhe', a.astype(v.dtype), v)

    out = lax.map(block, jnp.arange(S // Q_BLOCK))
    out = jnp.moveaxis(out, 0, 1).reshape(B, S, N_DIFF_HEADS, 2 * DIFF_HEAD_DIM)
    out = rmsnorm(out, sub_g) * (1.0 - lam_init)
    return out.reshape(B, S, ATTN_WIDTH)


def spatial_gating(u, s, ln_g, ln_b, w_s, b_s):
    B, S = u.shape[0], u.shape[1]
    v = layernorm(s, ln_g, ln_b)
    v = v.reshape(B, S // SGU_CHUNK, SGU_CHUNK, N_SGU_GROUPS, SGU_GROUP_DIM)
    mixed = jnp.einsum('gts,bnsgc->bntgc', w_s, v) + b_s.T[:, :, None]
    return u * mixed.reshape(B, S, SGU_WIDTH)


def hier_moe(h, w_rg, b_rg, w_re, b_re, w_gu, w_down):
    B, S, D = h.shape
    t = h.reshape(-1, D)
    T = t.shape[0]
    gp = jax.nn.softmax((t @ w_rg + b_rg).astype(jnp.float32), axis=-1)
    gval, gidx = lax.top_k(gp, 1)
    el = (t @ w_re + b_re).astype(jnp.float32).reshape(T, N_GROUPS, EXPERTS_PER_GROUP)
    el_sel = jnp.take_along_axis(el, gidx[:, :, None], axis=1)[:, 0]
    ep = jax.nn.softmax(el_sel, axis=-1)
    ev, eidx = lax.top_k(ep, TOP_K_IN_GROUP)
    ew = ev / jnp.sum(ev, axis=-1, keepdims=True) * gval
    gl_idx = gidx * EXPERTS_PER_GROUP + eidx
    combine = jnp.einsum('tk,tke->te', ew, jax.nn.one_hot(gl_idx, N_EXPERTS, dtype=jnp.float32))
    y = jnp.zeros((T, D), jnp.float32)
    for e in range(N_EXPERTS):
        gu = t @ w_gu[e]
        g, u = jnp.split(gu, 2, axis=-1)
        y = y + combine[:, e:e + 1] * ((jax.nn.silu(g) * u) @ w_down[e]).astype(jnp.float32)
    return y.astype(h.dtype).reshape(B, S, D)


def setup_inputs(seed: int = 0) -> dict:
    key = jax.random.key(seed)
    ks = jax.random.split(key, 32)
    f32 = jnp.float32
    L, D = DEPTH, D_MODEL
    nrm = lambda k, shape, s: jax.random.normal(k, shape, f32) * s
    return {
        "x": nrm(ks[0], (BATCH, SEQ, D), 1.0),
        "c": nrm(ks[1], (BATCH, D), 1.0),
        "w_ada": nrm(ks[2], (L, D, 6 * D), 0.5 * D ** -0.5),
        "b_ada": nrm(ks[3], (L, 6 * D), 0.02),
        "norm1_g": 1.0 + nrm(ks[4], (L, D), 0.02),
        "w_in": nrm(ks[5], (L, D, IN_WIDTH), D ** -0.5),
        "lambda_q1": nrm(ks[6], (L, DIFF_HEAD_DIM), 0.1),
        "lambda_k1": nrm(ks[7], (L, DIFF_HEAD_DIM), 0.1),
        "lambda_q2": nrm(ks[8], (L, DIFF_HEAD_DIM), 0.1),
        "lambda_k2": nrm(ks[9], (L, DIFF_HEAD_DIM), 0.1),
        "subln_g": 1.0 + nrm(ks[10], (L, 2 * DIFF_HEAD_DIM), 0.02),
        "w_attn_proj": nrm(ks[11], (L, ATTN_WIDTH, D), ATTN_WIDTH ** -0.5),
        "sgu_ln_g": 1.0 + nrm(ks[12], (L, SGU_WIDTH), 0.02),
        "sgu_ln_b": nrm(ks[13], (L, SGU_WIDTH), 0.02),
        "sgu_w_s": nrm(ks[14], (L, N_SGU_GROUPS, SGU_CHUNK, SGU_CHUNK), 0.5 * SGU_CHUNK ** -0.5),
        "sgu_b_s": 1.0 + nrm(ks[15], (L, N_SGU_GROUPS, SGU_CHUNK), 0.1),
        "w_sgu_proj": nrm(ks[16], (L, SGU_WIDTH, D), SGU_WIDTH ** -0.5),
        "w_out": nrm(ks[17], (L, D, D), D ** -0.5),
        "norm2_g": 1.0 + nrm(ks[18], (L, D), 0.02),
        "w_router_group": nrm(ks[19], (L, D, N_GROUPS), D ** -0.5),
        "b_router_group": nrm(ks[20], (L, N_GROUPS), 0.01),
        "w_router_expert": nrm(ks[21], (L, D, N_EXPERTS), D ** -0.5),
        "b_router_expert": nrm(ks[22], (L, N_EXPERTS), 0.01),
        "w_expert_gate_up": nrm(ks[23], (L, N_EXPERTS, D, 2 * D_EXPERT), D ** -0.5),
        "w_expert_down": nrm(ks[24], (L, N_EXPERTS, D_EXPERT, D), D_EXPERT ** -0.5),
        "final_g": 1.0 + nrm(ks[25], (D,), 0.02),
    }


def reference(x, c, w_ada, b_ada, norm1_g, w_in, lambda_q1, lambda_k1, lambda_q2, lambda_k2,
              subln_g, w_attn_proj, sgu_ln_g, sgu_ln_b, sgu_w_s, sgu_b_s, w_sgu_proj, w_out,
              norm2_g, w_router_group, b_router_group, w_router_expert, b_router_expert,
              w_expert_gate_up, w_expert_down, final_g):
    B, S = x.shape[0], x.shape[1]
    for l in range(DEPTH):
        mod = jax.nn.silu(c) @ w_ada[l] + b_ada[l]
        shift1, scale1, gate1, shift2, scale2, gate2 = jnp.split(mod, 6, axis=-1)

        h = modulate(rmsnorm(x, norm1_g[l]), shift1, scale1)
        q, k, v, u, s, ga, gb = jnp.split(h @ w_in[l], IN_SPLITS, axis=-1)
        q = q.reshape(B, S, N_DIFF_HEADS, 2, DIFF_HEAD_DIM)
        k = k.reshape(B, S, N_DIFF_HEADS, 2, DIFF_HEAD_DIM)
        v = v.reshape(B, S, N_DIFF_HEADS, 2 * DIFF_HEAD_DIM)
        lam_init = lambda_init_fn(l)
        lam = (jnp.exp(jnp.sum(lambda_q1[l] * lambda_k1[l]).astype(jnp.float32))
               - jnp.exp(jnp.sum(lambda_q2[l] * lambda_k2[l]).astype(jnp.float32)) + lam_init)
        y_attn = diff_attention(q, k, v, lam, subln_g[l], lam_init) @ w_attn_proj[l]
        y_sgu = spatial_gating(jax.nn.gelu(u), jax.nn.gelu(s), sgu_ln_g[l], sgu_ln_b[l],
                               sgu_w_s[l], sgu_b_s[l]) @ w_sgu_proj[l]
        y = jax.nn.sigmoid(ga) * y_attn + jax.nn.sigmoid(gb) * y_sgu
        x = x + gate1[:, None, :] * (y @ w_out[l])

        h2 = modulate(rmsnorm(x, norm2_g[l]), shift2, scale2)
        x = x + gate2[:, None, :] * hier_moe(h2, w_router_group[l], b_router_group[l],
                                             w_router_expert[l], b_router_expert[l],
                                             w_expert_gate_up[l], w_expert_down[l])
    return rmsnorm(x, final_g)
```

```python
import functools
import math

import jax
import jax.numpy as jnp
import numpy as np
from jax import lax
from jax.experimental import pallas as pl
from jax.experimental.pallas import tpu as pltpu

D_MODEL = 1024
N_HEADS = 8
HEAD_DIM = 64
HEAD_WIDTH = 2 * HEAD_DIM
N_SGU_GROUPS = 8
SGU_CHUNK = 128
N_GROUPS = 4
EXPERTS_PER_GROUP = 8
N_EXPERTS = N_GROUPS * EXPERTS_PER_GROUP
D_EXPERT = 256
N_SEGMENTS = 7
EPS = 1e-6
LAMBDA_INIT = 0.8 - 0.6 * math.exp(-0.3 * 0)
ALIBI_SLOPES = np.array([2.0 ** (-8.0 * (h + 1) / N_HEADS) for h in range(N_HEADS)], dtype=np.float32)
ROUTER_LANES = 128

F32 = jnp.float32
BF16 = jnp.bfloat16
VMEM_LIMIT_BYTES = 56 * 1024 * 1024


def _params(semantics):
    return pltpu.CompilerParams(dimension_semantics=semantics, vmem_limit_bytes=VMEM_LIMIT_BYTES)


def _dot(a, b):
    return jnp.dot(a, b, preferred_element_type=F32)


def _sigmoid(x):
    return 1.0 / (1.0 + jnp.exp(-x))


def _gelu_tanh(x):
    return 0.5 * x * (1.0 + jnp.tanh(math.sqrt(2.0 / math.pi) * (x + 0.044715 * (x * x * x))))


def _ada_kernel(c_ref, w_ref, b_ref, lq1_ref, lk1_ref, lq2_ref, lk2_ref, mod_ref, lam_ref):
    c = c_ref[...]
    act = c * _sigmoid(c)
    mod_ref[...] = jnp.dot(act, w_ref[...], preferred_element_type=F32,
                           precision=lax.Precision.HIGHEST) + b_ref[...]
    d1 = jnp.sum(lq1_ref[...] * lk1_ref[...], axis=-1, keepdims=True)
    d2 = jnp.sum(lq2_ref[...] * lk2_ref[...], axis=-1, keepdims=True)
    lam_ref[...] = jnp.exp(d1) - jnp.exp(d2) + LAMBDA_INIT


def _ada(c, w_ada, b_ada, lq1, lk1, lq2, lk2):
    batch, d = c.shape
    n = w_ada.shape[1]
    tn = 1024
    vec = pl.BlockSpec((1, HEAD_DIM), lambda j: (0, 0))
    return pl.pallas_call(
        _ada_kernel,
        grid=(n // tn,),
        in_specs=[pl.BlockSpec((batch, d), lambda j: (0, 0)),
                  pl.BlockSpec((d, tn), lambda j: (0, j)),
                  pl.BlockSpec((1, tn), lambda j: (0, j)),
                  vec, vec, vec, vec],
        out_specs=[pl.BlockSpec((batch, tn), lambda j: (0, j)),
                   pl.BlockSpec((1, 1), lambda j: (0, 0))],
        out_shape=[jax.ShapeDtypeStruct((batch, n), F32), jax.ShapeDtypeStruct((1, 1), F32)],
        compiler_params=_params(("arbitrary",)),
        name="ada",
    )(c, w_ada, b_ada.reshape(1, n), lq1.reshape(1, -1), lk1.reshape(1, -1),
      lq2.reshape(1, -1), lk2.reshape(1, -1))


def _in_proj_kernel(x_ref, mod_ref, g_ref, w_ref, o_ref, h_scr):
    j = pl.program_id(1)

    @pl.when(j == 0)
    def _():
        x = x_ref[...]
        y = x * lax.rsqrt(jnp.mean(x * x, axis=-1, keepdims=True) + EPS) * g_ref[...]
        h_scr[...] = (y * (1.0 + mod_ref[0, 1:2, :]) + mod_ref[0, 0:1, :]).astype(BF16)

    acc = _dot(h_scr[...], w_ref[...])

    @pl.when(j == 0)
    def _():
        o_ref[0] = (acc * (HEAD_DIM ** -0.5)).astype(BF16)

    @pl.when((j == 1) | (j == 2))
    def _():
        o_ref[0] = acc.astype(BF16)

    @pl.when((j == 3) | (j == 4))
    def _():
        o_ref[0] = _gelu_tanh(acc).astype(BF16)

    @pl.when(j >= 5)
    def _():
        o_ref[0] = _sigmoid(acc).astype(BF16)


def _in_proj(x2d, mod3, norm_g, w_in_bf, seq):
    t, d = x2d.shape
    tm = 1024
    tiles_per_batch = seq // tm
    return pl.pallas_call(
        _in_proj_kernel,
        grid=(t // tm, N_SEGMENTS),
        in_specs=[pl.BlockSpec((tm, d), lambda i, j: (i, 0)),
                  pl.BlockSpec((1, 6, d), lambda i, j: (i // tiles_per_batch, 0, 0)),
                  pl.BlockSpec((1, d), lambda i, j: (0, 0)),
                  pl.BlockSpec((d, d), lambda i, j: (0, j))],
        out_specs=pl.BlockSpec((1, tm, d), lambda i, j: (j, i, 0)),
        out_shape=jax.ShapeDtypeStruct((N_SEGMENTS, t, d), BF16),
        scratch_shapes=[pltpu.VMEM((tm, d), BF16)],
        compiler_params=_params(("arbitrary", "arbitrary")),
        name="in_proj",
    )(x2d, mod3, norm_g.reshape(1, d), w_in_bf)


def _attn_kernel(slopes_ref, lam_ref, q_ref, k_ref, v_ref, subg_ref, o_ref, bias_scr, *, tq, seq):
    h = pl.program_id(0)
    qb = pl.program_id(1)
    b = pl.program_id(2)

    @pl.when(b == 0)
    def _():
        qpos = qb * tq + lax.broadcasted_iota(jnp.int32, (tq, seq), 0)
        kpos = lax.broadcasted_iota(jnp.int32, (tq, seq), 1)
        bias_scr[...] = -slopes_ref[h] * jnp.abs(qpos - kpos).astype(F32)

    q = q_ref[0]
    k = k_ref[0]
    v = v_ref[0]
    lane = lax.broadcasted_iota(jnp.int32, q.shape, 1)
    zero = jnp.zeros_like(q)
    q0 = jnp.where(lane < HEAD_DIM, q, zero)
    q1 = jnp.where(lane >= HEAD_DIM, q, zero)
    contract_last = (((1,), (1,)), ((), ()))
    bias = bias_scr[...]
    s0 = lax.dot_general(q0, k, contract_last, preferred_element_type=F32) + bias
    s1 = lax.dot_general(q1, k, contract_last, preferred_element_type=F32) + bias
    p0 = jnp.exp(s0 - jnp.max(s0, axis=-1, keepdims=True))
    p1 = jnp.exp(s1 - jnp.max(s1, axis=-1, keepdims=True))
    l0 = jnp.sum(p0, axis=-1, keepdims=True)
    l1 = jnp.sum(p1, axis=-1, keepdims=True)
    a = p0 * (1.0 / l0) - p1 * (lam_ref[0, 0] / l1)
    o = _dot(a.astype(BF16), v)
    o = o * lax.rsqrt(jnp.mean(o * o, axis=-1, keepdims=True) + EPS)
    o_ref[...] = (o * subg_ref[...] * (1.0 - LAMBDA_INIT)).astype(BF16)


def _attention(proj, lam, subln_g, batch, seq):
    t = batch * seq
    tq = 256
    nq = seq // tq
    kernel = functools.partial(_attn_kernel, tq=tq, seq=seq)
    smem = pl.BlockSpec(memory_space=pltpu.SMEM)
    return pl.pallas_call(
        kernel,
        grid=(N_HEADS, nq, batch),
        in_specs=[smem, smem,
                  pl.BlockSpec((1, tq, HEAD_WIDTH), lambda h, qb, b: (0, b * nq + qb, h)),
                  pl.BlockSpec((1, seq, HEAD_WIDTH), lambda h, qb, b: (1, b, h)),
                  pl.BlockSpec((1, seq, HEAD_WIDTH), lambda h, qb, b: (2, b, h)),
                  pl.BlockSpec((1, HEAD_WIDTH), lambda h, qb, b: (0, 0))],
        out_specs=pl.BlockSpec((tq, HEAD_WIDTH), lambda h, qb, b: (b * nq + qb, h)),
        out_shape=jax.ShapeDtypeStruct((t, N_HEADS * HEAD_WIDTH), BF16),
        scratch_shapes=[pltpu.VMEM((tq, seq), F32)],
        compiler_params=_params(("arbitrary", "arbitrary", "arbitrary")),
        name="diff_attn",
    )(jnp.asarray(ALIBI_SLOPES), lam, proj, proj, proj, subln_g.reshape(1, HEAD_WIDTH))


def _mix_kernel(attn_ref, u_ref, s_ref, ga_ref, gb_ref, x_ref, mod_ref,
                wap_ref, lng_ref, lnb_ref, ws_ref, bs_ref, wsp_ref, wout_ref,
                n2g_ref, wr_hi_ref, wr_lo_ref, br_ref,
                x1_ref, h2_ref, logit_ref, gated_scr):
    tm = x_ref.shape[0]
    y_attn = _dot(attn_ref[...], wap_ref[...])

    s = s_ref[0].astype(F32)
    mu = jnp.mean(s, axis=-1, keepdims=True)
    sc = s - mu
    var = jnp.mean(sc * sc, axis=-1, keepdims=True)
    v = ((sc * lax.rsqrt(var + EPS)) * lng_ref[...] + lnb_ref[...]).astype(BF16)
    for c in range(tm // SGU_CHUNK):
        rows = slice(c * SGU_CHUNK, (c + 1) * SGU_CHUNK)
        for g in range(N_SGU_GROUPS):
            cols = slice(g * SGU_CHUNK, (g + 1) * SGU_CHUNK)
            mixed = _dot(ws_ref[g], v[rows, cols]) + bs_ref[g]
            gated_scr[rows, cols] = (u_ref[0, rows, cols].astype(F32) * mixed).astype(BF16)
    y_sgu = _dot(gated_scr[...], wsp_ref[...])

    y = ga_ref[0].astype(F32) * y_attn + gb_ref[0].astype(F32) * y_sgu
    x1 = x_ref[...] + mod_ref[0, 2:3, :] * _dot(y.astype(BF16), wout_ref[...])
    x1_ref[...] = x1

    h2 = x1 * lax.rsqrt(jnp.mean(x1 * x1, axis=-1, keepdims=True) + EPS) * n2g_ref[...]
    h2 = h2 * (1.0 + mod_ref[0, 4:5, :]) + mod_ref[0, 3:4, :]
    h2_hi = h2.astype(BF16)
    h2_ref[...] = h2_hi
    h2_lo = (h2 - h2_hi.astype(F32)).astype(BF16)
    logit_ref[...] = (_dot(h2_hi, wr_hi_ref[...]) + _dot(h2_lo, wr_hi_ref[...])
                      + _dot(h2_hi, wr_lo_ref[...]) + br_ref[...])


def _mix(attn, proj, x2d, mod3, wap, lng, lnb, ws, bs_full, wsp, wout, n2g, wr_hi, wr_lo, br, seq):
    t, d = x2d.shape
    tm = 256
    tiles_per_batch = seq // tm
    const2 = lambda shape: pl.BlockSpec(shape, lambda i: (0, 0))
    const3 = lambda shape: pl.BlockSpec(shape, lambda i: (0, 0, 0))
    seg = lambda k: pl.BlockSpec((1, tm, d), lambda i, k=k: (k, i, 0))
    row = pl.BlockSpec((tm, d), lambda i: (i, 0))
    return pl.pallas_call(
        _mix_kernel,
        grid=(t // tm,),
        in_specs=[row, seg(3), seg(4), seg(5), seg(6), row,
                  pl.BlockSpec((1, 6, d), lambda i: (i // tiles_per_batch, 0, 0)),
                  const2((d, d)), const2((1, d)), const2((1, d)),
                  const3((N_SGU_GROUPS, SGU_CHUNK, SGU_CHUNK)),
                  const3((N_SGU_GROUPS, SGU_CHUNK, SGU_CHUNK)),
                  const2((d, d)), const2((d, d)), const2((1, d)),
                  const2((d, ROUTER_LANES)), const2((d, ROUTER_LANES)), const2((1, ROUTER_LANES))],
        out_specs=[row, row, pl.BlockSpec((tm, ROUTER_LANES), lambda i: (i, 0))],
        out_shape=[jax.ShapeDtypeStruct((t, d), F32), jax.ShapeDtypeStruct((t, d), BF16),
                   jax.ShapeDtypeStruct((t, ROUTER_LANES), F32)],
        scratch_shapes=[pltpu.VMEM((tm, d), BF16)],
        compiler_params=_params(("arbitrary",)),
        name="mix",
    )(attn, proj, proj, proj, proj, x2d, mod3, wap, lng, lnb, ws, bs_full, wsp, wout, n2g, wr_hi, wr_lo, br)


def _route_kernel(logit_ref, comb_ref):
    z = logit_ref[...]
    lane = lax.broadcasted_iota(jnp.int32, z.shape, 1)
    neg = jnp.float32(-jnp.inf)
    big = jnp.int32(ROUTER_LANES)

    def first_argmax(val, vmax):
        return jnp.min(jnp.where(val == vmax, lane, big), axis=-1, keepdims=True)

    is_group = (lane >= N_EXPERTS) & (lane < N_EXPERTS + N_GROUPS)
    gl = jnp.where(is_group, z, neg)
    gmax = jnp.max(gl, axis=-1, keepdims=True)
    ge = jnp.exp(gl - gmax)
    gp = ge / jnp.sum(ge, axis=-1, keepdims=True)
    gval = jnp.max(gp, axis=-1, keepdims=True)
    gidx = first_argmax(gp, gval) - N_EXPERTS

    in_group = (lane >= gidx * EXPERTS_PER_GROUP) & (lane < (gidx + 1) * EXPERTS_PER_GROUP)
    el = jnp.where(in_group, z, neg)
    emax = jnp.max(el, axis=-1, keepdims=True)
    ee = jnp.exp(el - emax)
    ep = ee / jnp.sum(ee, axis=-1, keepdims=True)
    ep = jnp.where(in_group, ep, -1.0)
    ev0 = jnp.max(ep, axis=-1, keepdims=True)
    ei0 = first_argmax(ep, ev0)
    ep_rest = jnp.where(lane == ei0, -1.0, ep)
    ev1 = jnp.max(ep_rest, axis=-1, keepdims=True)
    ei1 = first_argmax(ep_rest, ev1)
    denom = ev0 + ev1
    comb_ref[...] = jnp.where(lane == ei0, ev0 / denom * gval,
                              jnp.where(lane == ei1, ev1 / denom * gval, 0.0))


def _route(logits):
    t = logits.shape[0]
    tm = 1024
    spec = pl.BlockSpec((tm, ROUTER_LANES), lambda i: (i, 0))
    return pl.pallas_call(
        _route_kernel, grid=(t // tm,), in_specs=[spec], out_specs=spec,
        out_shape=jax.ShapeDtypeStruct((t, ROUTER_LANES), F32),
        compiler_params=_params(("arbitrary",)), name="route",
    )(logits)


def _moe_kernel(h2_ref, comb_ref, wgu_ref, wd_ref, x1_ref, mod_ref, fg_ref, out_ref, acc_scr):
    e = pl.program_id(1)

    @pl.when(e == 0)
    def _():
        acc_scr[...] = jnp.zeros_like(acc_scr)

    gu = _dot(h2_ref[...], wgu_ref[0])
    g = gu[:, :D_EXPERT]
    u = gu[:, D_EXPERT:]
    comb = comb_ref[...]
    lane = lax.broadcasted_iota(jnp.int32, comb.shape, 1)
    w = jnp.sum(jnp.where(lane == e, comb, 0.0), axis=-1, keepdims=True)
    act = (g * _sigmoid(g) * u * w).astype(BF16)
    acc_scr[...] += _dot(act, wd_ref[0])

    @pl.when(e == N_EXPERTS - 1)
    def _():
        x2 = x1_ref[...] + mod_ref[0, 5:6, :] * acc_scr[...]
        out_ref[...] = x2 * lax.rsqrt(jnp.mean(x2 * x2, axis=-1, keepdims=True) + EPS) * fg_ref[...]


def _moe(h2, comb, wgu, wd, x1, mod3, final_g, seq):
    t, d = x1.shape
    tm = 1024
    tiles_per_batch = seq // tm
    row = pl.BlockSpec((tm, d), lambda i, e: (i, 0))
    return pl.pallas_call(
        _moe_kernel,
        grid=(t // tm, N_EXPERTS),
        in_specs=[row,
                  pl.BlockSpec((tm, ROUTER_LANES), lambda i, e: (i, 0)),
                  pl.BlockSpec((1, d, 2 * D_EXPERT), lambda i, e: (e, 0, 0)),
                  pl.BlockSpec((1, D_EXPERT, d), lambda i, e: (e, 0, 0)),
                  row,
                  pl.BlockSpec((1, 6, d), lambda i, e: (i // tiles_per_batch, 0, 0)),
                  pl.BlockSpec((1, d), lambda i, e: (0, 0))],
        out_specs=row,
        out_shape=jax.ShapeDtypeStruct((t, d), F32),
        scratch_shapes=[pltpu.VMEM((tm, d), F32)],
        compiler_params=_params(("arbitrary", "arbitrary")),
        name="moe",
    )(h2, comb, wgu, wd, x1, mod3, final_g.reshape(1, d))


def _router_weights(w_rg, b_rg, w_re, b_re):
    d = w_rg.shape[0]
    pad = ROUTER_LANES - N_EXPERTS - N_GROUPS
    w = jnp.concatenate([w_re, w_rg, jnp.zeros((d, pad), F32)], axis=1)
    b = jnp.concatenate([b_re, b_rg, jnp.zeros((pad,), F32)]).reshape(1, ROUTER_LANES)
    w_hi = w.astype(BF16)
    w_lo = (w - w_hi.astype(F32)).astype(BF16)
    return w_hi, w_lo, b


def kernel(x, c, w_ada, b_ada, norm1_g, w_in, lambda_q1, lambda_k1, lambda_q2, lambda_k2, subln_g, w_attn_proj, sgu_ln_g, sgu_ln_b, sgu_w_s, sgu_b_s, w_sgu_proj, w_out, norm2_g, w_router_group, b_router_group, w_router_expert, b_router_expert, w_expert_gate_up, w_expert_down, final_g):
    batch, seq, d = x.shape
    assert w_ada.shape[0] == 1, "single-layer trunk"
    x2d = x.reshape(batch * seq, d)

    mod, lam = _ada(c, w_ada[0], b_ada[0], lambda_q1[0], lambda_k1[0], lambda_q2[0], lambda_k2[0])
    mod3 = mod.reshape(batch, 6, d)

    proj = _in_proj(x2d, mod3, norm1_g[0], w_in[0].astype(BF16), seq)
    attn = _attention(proj, lam, subln_g[0], batch, seq)

    bs_full = jnp.broadcast_to(sgu_b_s[0][:, :, None], (N_SGU_GROUPS, SGU_CHUNK, SGU_CHUNK))
    wr_hi, wr_lo, br = _router_weights(w_router_group[0], b_router_group[0],
                                       w_router_expert[0], b_router_expert[0])
    x1, h2, logits = _mix(attn, proj, x2d, mod3, w_attn_proj[0].astype(BF16),
                          sgu_ln_g[0].reshape(1, d), sgu_ln_b[0].reshape(1, d),
                          sgu_w_s[0].astype(BF16), bs_full, w_sgu_proj[0].astype(BF16),
                          w_out[0].astype(BF16), norm2_g[0].reshape(1, d), wr_hi, wr_lo, br, seq)
    comb = _route(logits)
    out = _moe(h2, comb, w_expert_gate_up[0].astype(BF16), w_expert_down[0].astype(BF16),
               x1, mod3, final_g, seq)
    return out.reshape(batch, seq, d)
```

```python
import functools
import math

import jax
import jax.numpy as jnp
import numpy as np
from jax import lax
from jax.experimental import pallas as pl
from jax.experimental.pallas import tpu as pltpu

D_MODEL = 1024
N_HEADS = 8
HEAD_DIM = 64
HEAD_WIDTH = 2 * HEAD_DIM
N_SGU_GROUPS = 8
SGU_CHUNK = 128
N_GROUPS = 4
EXPERTS_PER_GROUP = 8
N_EXPERTS = N_GROUPS * EXPERTS_PER_GROUP
D_EXPERT = 256
N_SEGMENTS = 7
EPS = 1e-6
LAMBDA_INIT = 0.8 - 0.6 * math.exp(-0.3 * 0)
ALIBI_SLOPES = np.array([2.0 ** (-8.0 * (h + 1) / N_HEADS) for h in range(N_HEADS)], dtype=np.float32)
ROUTER_LANES = 128
LOG2_E = math.log2(math.e)
ATTN_ROWS = 256
ATTN_KEY_TILE = 256
ATTN_L_MIN = 2.0 ** -60
ATTN_L_MAX = 2.0 ** 100

F32 = jnp.float32
BF16 = jnp.bfloat16
VMEM_LIMIT_BYTES = 56 * 1024 * 1024


def _params(semantics):
    return pltpu.CompilerParams(dimension_semantics=semantics, vmem_limit_bytes=VMEM_LIMIT_BYTES)


def _dot(a, b):
    return jnp.dot(a, b, preferred_element_type=F32)


def _sigmoid(x):
    return 1.0 / (1.0 + jnp.exp(-x))


def _gelu_tanh(x):
    return 0.5 * x * (1.0 + jnp.tanh(math.sqrt(2.0 / math.pi) * (x + 0.044715 * (x * x * x))))


def _ada_kernel(c_ref, w_ref, b_ref, lq1_ref, lk1_ref, lq2_ref, lk2_ref, mod_ref, lam_ref):
    c = c_ref[...]
    act = c * _sigmoid(c)
    mod_ref[...] = jnp.dot(act, w_ref[...], preferred_element_type=F32,
                           precision=lax.Precision.HIGHEST) + b_ref[...]
    d1 = jnp.sum(lq1_ref[...] * lk1_ref[...], axis=-1, keepdims=True)
    d2 = jnp.sum(lq2_ref[...] * lk2_ref[...], axis=-1, keepdims=True)
    lam_ref[...] = jnp.exp(d1) - jnp.exp(d2) + LAMBDA_INIT


def _ada(c, w_ada, b_ada, lq1, lk1, lq2, lk2):
    batch, d = c.shape
    n = w_ada.shape[1]
    tn = 1024
    vec = pl.BlockSpec((1, HEAD_DIM), lambda j: (0, 0))
    return pl.pallas_call(
        _ada_kernel,
        grid=(n // tn,),
        in_specs=[pl.BlockSpec((batch, d), lambda j: (0, 0)),
                  pl.BlockSpec((d, tn), lambda j: (0, j)),
                  pl.BlockSpec((1, tn), lambda j: (0, j)),
                  vec, vec, vec, vec],
        out_specs=[pl.BlockSpec((batch, tn), lambda j: (0, j)),
                   pl.BlockSpec((1, 1), lambda j: (0, 0))],
        out_shape=[jax.ShapeDtypeStruct((batch, n), F32), jax.ShapeDtypeStruct((1, 1), F32)],
        compiler_params=_params(("arbitrary",)),
        name="ada",
    )(c, w_ada, b_ada.reshape(1, n), lq1.reshape(1, -1), lk1.reshape(1, -1),
      lq2.reshape(1, -1), lk2.reshape(1, -1))


def _in_proj_kernel(x_ref, mod_ref, g_ref, w_ref, o_ref, h_scr):
    j = pl.program_id(1)

    @pl.when(j == 0)
    def _():
        x = x_ref[...]
        y = x * lax.rsqrt(jnp.mean(x * x, axis=-1, keepdims=True) + EPS) * g_ref[...]
        h_scr[...] = (y * (1.0 + mod_ref[0, 1:2, :]) + mod_ref[0, 0:1, :]).astype(BF16)

    acc = _dot(h_scr[...], w_ref[...])

    @pl.when(j == 0)
    def _():
        o_ref[0] = (acc * (HEAD_DIM ** -0.5 * LOG2_E)).astype(BF16)

    @pl.when((j == 1) | (j == 2))
    def _():
        o_ref[0] = acc.astype(BF16)

    @pl.when((j == 3) | (j == 4))
    def _():
        o_ref[0] = _gelu_tanh(acc).astype(BF16)

    @pl.when(j >= 5)
    def _():
        o_ref[0] = _sigmoid(acc).astype(BF16)


def _in_proj(x2d, mod3, norm_g, w_in_bf, seq):
    t, d = x2d.shape
    tm = 1024
    tiles_per_batch = seq // tm
    return pl.pallas_call(
        _in_proj_kernel,
        grid=(t // tm, N_SEGMENTS),
        in_specs=[pl.BlockSpec((tm, d), lambda i, j: (i, 0)),
                  pl.BlockSpec((1, 6, d), lambda i, j: (i // tiles_per_batch, 0, 0)),
                  pl.BlockSpec((1, d), lambda i, j: (0, 0)),
                  pl.BlockSpec((d, d), lambda i, j: (0, j))],
        out_specs=pl.BlockSpec((1, tm, d), lambda i, j: (j, i, 0)),
        out_shape=jax.ShapeDtypeStruct((N_SEGMENTS, t, d), BF16),
        scratch_shapes=[pltpu.VMEM((tm, d), BF16)],
        compiler_params=_params(("arbitrary", "arbitrary")),
        name="in_proj",
    )(x2d, mod3, norm_g.reshape(1, d), w_in_bf)


def _attn_epilogue(o0, o1, lam, subg):
    o = o0 - lam * o1
    o = o * lax.rsqrt(jnp.mean(o * o, axis=-1, keepdims=True) + EPS)
    return (o * subg * (1.0 - LAMBDA_INIT)).astype(BF16)


def _attn_kernel(slopes_ref, lam_ref, q_ref, k_ref, v_ref, subg_ref, o_ref,
                 bias_scr, p_scr, vext_scr, *, rows, seq):
    h = pl.program_id(0)
    b = pl.program_id(1)
    nq = seq // rows
    contract_last = (((1,), (1,)), ((), ()))

    @pl.when(b == 0)
    def _():
        slope = -LOG2_E * slopes_ref[h]
        for qb in range(nq):
            qpos = qb * rows + lax.broadcasted_iota(jnp.int32, (rows, seq), 0)
            kpos = lax.broadcasted_iota(jnp.int32, (rows, seq), 1)
            bias_scr[qb] = slope * jnp.abs(qpos - kpos).astype(F32)
        lane = lax.broadcasted_iota(jnp.int32, (seq, HEAD_WIDTH), 1)
        vext_scr[:, HEAD_WIDTH:] = jnp.where(lane == 0, 1.0, 0.0).astype(BF16)

    vext_scr[:, :HEAD_WIDTH] = v_ref[0]
    lam = lam_ref[0, 0]
    subg = subg_ref[...]

    def masked_q(r0):
        q = q_ref[0, pl.ds(r0, rows), :]
        lane = lax.broadcasted_iota(jnp.int32, q.shape, 1)
        zero = jnp.zeros_like(q)
        return jnp.where(lane < HEAD_DIM, q, zero), jnp.where(lane >= HEAD_DIM, q, zero)

    def fast_block(qb, n_bad):
        r0 = pl.multiple_of(qb * rows, rows)
        q0, q1 = masked_q(r0)
        qq = jnp.concatenate([q0, q1], axis=0)
        for c in range(seq // ATTN_KEY_TILE):
            ks = slice(c * ATTN_KEY_TILE, (c + 1) * ATTN_KEY_TILE)
            s = lax.dot_general(qq, k_ref[0, ks, :], contract_last, preferred_element_type=F32)
            bias = bias_scr[qb, :, ks]
            p_scr[:rows, ks] = jnp.exp2(s[:rows] + bias).astype(BF16)
            p_scr[rows:, ks] = jnp.exp2(s[rows:] + bias).astype(BF16)
        oo = _dot(p_scr[...], vext_scr[...])
        l0 = oo[:rows, HEAD_WIDTH:HEAD_WIDTH + 1]
        l1 = oo[rows:, HEAD_WIDTH:HEAD_WIDTH + 1]
        o_ref[pl.ds(r0, rows), :] = _attn_epilogue(oo[:rows, :HEAD_WIDTH] / l0, oo[rows:, :HEAD_WIDTH] / l1,
                                                   lam, subg)
        ok = ((l0 >= ATTN_L_MIN) & (l0 <= ATTN_L_MAX)) & ((l1 >= ATTN_L_MIN) & (l1 <= ATTN_L_MAX))
        return n_bad + jnp.where(ok, 0.0, 1.0)

    n_bad = lax.fori_loop(0, nq, fast_block, jnp.zeros((rows, 1), F32), unroll=4)

    @pl.when(jnp.sum(n_bad) > 0.0)
    def _():
        def safe_block(qb, carry):
            r0 = pl.multiple_of(qb * rows, rows)
            outs = []
            for qm in masked_q(r0):
                s = lax.dot_general(qm, k_ref[0], contract_last, preferred_element_type=F32) + bias_scr[qb]
                p = jnp.exp2(s - jnp.max(s, axis=-1, keepdims=True))
                outs.append(_dot(p.astype(BF16), v_ref[0]) / jnp.sum(p, axis=-1, keepdims=True))
            o_ref[pl.ds(r0, rows), :] = _attn_epilogue(outs[0], outs[1], lam, subg)
            return carry

        lax.fori_loop(0, nq, safe_block, 0)


def _attention(proj, lam, subln_g, batch, seq):
    t = batch * seq
    rows = ATTN_ROWS
    kernel = functools.partial(_attn_kernel, rows=rows, seq=seq)
    smem = pl.BlockSpec(memory_space=pltpu.SMEM)
    seg = lambda k: pl.BlockSpec((1, seq, HEAD_WIDTH), lambda h, b, k=k: (k, b, h))
    return pl.pallas_call(
        kernel,
        grid=(N_HEADS, batch),
        in_specs=[smem, smem, seg(0), seg(1), seg(2),
                  pl.BlockSpec((1, HEAD_WIDTH), lambda h, b: (0, 0))],
        out_specs=pl.BlockSpec((seq, HEAD_WIDTH), lambda h, b: (b, h)),
        out_shape=jax.ShapeDtypeStruct((t, N_HEADS * HEAD_WIDTH), BF16),
        scratch_shapes=[pltpu.VMEM((seq // rows, rows, seq), F32),
                        pltpu.VMEM((2 * rows, seq), BF16),
                        pltpu.VMEM((seq, 2 * HEAD_WIDTH), BF16)],
        compiler_params=_params(("arbitrary", "arbitrary")),
        name="diff_attn",
    )(jnp.asarray(ALIBI_SLOPES), lam, proj, proj, proj, subln_g.reshape(1, HEAD_WIDTH))


def _mix_kernel(attn_ref, u_ref, s_ref, ga_ref, gb_ref, x_ref, mod_ref,
                wap_ref, lng_ref, lnb_ref, ws_ref, bs_ref, wsp_ref, wout_ref,
                n2g_ref, wr_hi_ref, wr_lo_ref, br_ref,
                x1_ref, h2_ref, logit_ref, gated_scr):
    tm = x_ref.shape[0]
    y_attn = _dot(attn_ref[...], wap_ref[...])

    s = s_ref[0].astype(F32)
    mu = jnp.mean(s, axis=-1, keepdims=True)
    sc = s - mu
    var = jnp.mean(sc * sc, axis=-1, keepdims=True)
    v = ((sc * lax.rsqrt(var + EPS)) * lng_ref[...] + lnb_ref[...]).astype(BF16)
    for c in range(tm // SGU_CHUNK):
        rows = slice(c * SGU_CHUNK, (c + 1) * SGU_CHUNK)
        for g in range(N_SGU_GROUPS):
            cols = slice(g * SGU_CHUNK, (g + 1) * SGU_CHUNK)
            mixed = _dot(ws_ref[g], v[rows, cols]) + bs_ref[g]
            gated_scr[rows, cols] = (u_ref[0, rows, cols].astype(F32) * mixed).astype(BF16)
    y_sgu = _dot(gated_scr[...], wsp_ref[...])

    y = ga_ref[0].astype(F32) * y_attn + gb_ref[0].astype(F32) * y_sgu
    x1 = x_ref[...] + mod_ref[0, 2:3, :] * _dot(y.astype(BF16), wout_ref[...])
    x1_ref[...] = x1

    h2 = x1 * lax.rsqrt(jnp.mean(x1 * x1, axis=-1, keepdims=True) + EPS) * n2g_ref[...]
    h2 = h2 * (1.0 + mod_ref[0, 4:5, :]) + mod_ref[0, 3:4, :]
    h2_hi = h2.astype(BF16)
    h2_ref[...] = h2_hi
    h2_lo = (h2 - h2_hi.astype(F32)).astype(BF16)
    logit_ref[...] = (_dot(h2_hi, wr_hi_ref[...]) + _dot(h2_lo, wr_hi_ref[...])
                      + _dot(h2_hi, wr_lo_ref[...]) + br_ref[...])


def _mix(attn, proj, x2d, mod3, wap, lng, lnb, ws, bs_full, wsp, wout, n2g, wr_hi, wr_lo, br, seq):
    t, d = x2d.shape
    tm = 256
    tiles_per_batch = seq // tm
    const2 = lambda shape: pl.BlockSpec(shape, lambda i: (0, 0))
    const3 = lambda shape: pl.BlockSpec(shape, lambda i: (0, 0, 0))
    seg = lambda k: pl.BlockSpec((1, tm, d), lambda i, k=k: (k, i, 0))
    row = pl.BlockSpec((tm, d), lambda i: (i, 0))
    return pl.pallas_call(
        _mix_kernel,
        grid=(t // tm,),
        in_specs=[row, seg(3), seg(4), seg(5), seg(6), row,
                  pl.BlockSpec((1, 6, d), lambda i: (i // tiles_per_batch, 0, 0)),
                  const2((d, d)), const2((1, d)), const2((1, d)),
                  const3((N_SGU_GROUPS, SGU_CHUNK, SGU_CHUNK)),
                  const3((N_SGU_GROUPS, SGU_CHUNK, SGU_CHUNK)),
                  const2((d, d)), const2((d, d)), const2((1, d)),
                  const2((d, ROUTER_LANES)), const2((d, ROUTER_LANES)), const2((1, ROUTER_LANES))],
        out_specs=[row, row, pl.BlockSpec((tm, ROUTER_LANES), lambda i: (i, 0))],
        out_shape=[jax.ShapeDtypeStruct((t, d), F32), jax.ShapeDtypeStruct((t, d), BF16),
                   jax.ShapeDtypeStruct((t, ROUTER_LANES), F32)],
        scratch_shapes=[pltpu.VMEM((tm, d), BF16)],
        compiler_params=_params(("arbitrary",)),
        name="mix",
    )(attn, proj, proj, proj, proj, x2d, mod3, wap, lng, lnb, ws, bs_full, wsp, wout, n2g, wr_hi, wr_lo, br)


def _route_kernel(logit_ref, comb_ref):
    z = logit_ref[...]
    lane = lax.broadcasted_iota(jnp.int32, z.shape, 1)
    neg = jnp.float32(-jnp.inf)
    big = jnp.int32(ROUTER_LANES)

    def first_argmax(val, vmax):
        return jnp.min(jnp.where(val == vmax, lane, big), axis=-1, keepdims=True)

    is_group = (lane >= N_EXPERTS) & (lane < N_EXPERTS + N_GROUPS)
    gl = jnp.where(is_group, z, neg)
    gmax = jnp.max(gl, axis=-1, keepdims=True)
    ge = jnp.exp(gl - gmax)
    gp = ge / jnp.sum(ge, axis=-1, keepdims=True)
    gval = jnp.max(gp, axis=-1, keepdims=True)
    gidx = first_argmax(gp, gval) - N_EXPERTS

    in_group = (lane >= gidx * EXPERTS_PER_GROUP) & (lane < (gidx + 1) * EXPERTS_PER_GROUP)
    el = jnp.where(in_group, z, neg)
    emax = jnp.max(el, axis=-1, keepdims=True)
    ee = jnp.exp(el - emax)
    ep = ee / jnp.sum(ee, axis=-1, keepdims=True)
    ep = jnp.where(in_group, ep, -1.0)
    ev0 = jnp.max(ep, axis=-1, keepdims=True)
    ei0 = first_argmax(ep, ev0)
    ep_rest = jnp.where(lane == ei0, -1.0, ep)
    ev1 = jnp.max(ep_rest, axis=-1, keepdims=True)
    ei1 = first_argmax(ep_rest, ev1)
    denom = ev0 + ev1
    comb_ref[...] = jnp.where(lane == ei0, ev0 / denom * gval,
                              jnp.where(lane == ei1, ev1 / denom * gval, 0.0))


def _route(logits):
    t = logits.shape[0]
    tm = 1024
    spec = pl.BlockSpec((tm, ROUTER_LANES), lambda i: (i, 0))
    return pl.pallas_call(
        _route_kernel, grid=(t // tm,), in_specs=[spec], out_specs=spec,
        out_shape=jax.ShapeDtypeStruct((t, ROUTER_LANES), F32),
        compiler_params=_params(("arbitrary",)), name="route",
    )(logits)


def _moe_kernel(h2_ref, comb_ref, wgu_ref, wd_ref, x1_ref, mod_ref, fg_ref, out_ref, acc_scr):
    e = pl.program_id(1)

    @pl.when(e == 0)
    def _():
        acc_scr[...] = jnp.zeros_like(acc_scr)

    gu = _dot(h2_ref[...], wgu_ref[0])
    g = gu[:, :D_EXPERT]
    u = gu[:, D_EXPERT:]
    comb = comb_ref[...]
    lane = lax.broadcasted_iota(jnp.int32, comb.shape, 1)
    w = jnp.sum(jnp.where(lane == e, comb, 0.0), axis=-1, keepdims=True)
    act = (g * _sigmoid(g) * u * w).astype(BF16)
    acc_scr[...] += _dot(act, wd_ref[0])

    @pl.when(e == N_EXPERTS - 1)
    def _():
        x2 = x1_ref[...] + mod_ref[0, 5:6, :] * acc_scr[...]
        out_ref[...] = x2 * lax.rsqrt(jnp.mean(x2 * x2, axis=-1, keepdims=True) + EPS) * fg_ref[...]


def _moe(h2, comb, wgu, wd, x1, mod3, final_g, seq):
    t, d = x1.shape
    tm = 1024
    tiles_per_batch = seq // tm
    row = pl.BlockSpec((tm, d), lambda i, e: (i, 0))
    return pl.pallas_call(
        _moe_kernel,
        grid=(t // tm, N_EXPERTS),
        in_specs=[row,
                  pl.BlockSpec((tm, ROUTER_LANES), lambda i, e: (i, 0)),
                  pl.BlockSpec((1, d, 2 * D_EXPERT), lambda i, e: (e, 0, 0)),
                  pl.BlockSpec((1, D_EXPERT, d), lambda i, e: (e, 0, 0)),
                  row,
                  pl.BlockSpec((1, 6, d), lambda i, e: (i // tiles_per_batch, 0, 0)),
                  pl.BlockSpec((1, d), lambda i, e: (0, 0))],
        out_specs=row,
        out_shape=jax.ShapeDtypeStruct((t, d), F32),
        scratch_shapes=[pltpu.VMEM((tm, d), F32)],
        compiler_params=_params(("arbitrary", "arbitrary")),
        name="moe",
    )(h2, comb, wgu, wd, x1, mod3, final_g.reshape(1, d))


def _router_weights(w_rg, b_rg, w_re, b_re):
    d = w_rg.shape[0]
    pad = ROUTER_LANES - N_EXPERTS - N_GROUPS
    w = jnp.concatenate([w_re, w_rg, jnp.zeros((d, pad), F32)], axis=1)
    b = jnp.concatenate([b_re, b_rg, jnp.zeros((pad,), F32)]).reshape(1, ROUTER_LANES)
    w_hi = w.astype(BF16)
    w_lo = (w - w_hi.astype(F32)).astype(BF16)
    return w_hi, w_lo, b


def kernel(x, c, w_ada, b_ada, norm1_g, w_in, lambda_q1, lambda_k1, lambda_q2, lambda_k2, subln_g, w_attn_proj, sgu_ln_g, sgu_ln_b, sgu_w_s, sgu_b_s, w_sgu_proj, w_out, norm2_g, w_router_group, b_router_group, w_router_expert, b_router_expert, w_expert_gate_up, w_expert_down, final_g):
    batch, seq, d = x.shape
    assert w_ada.shape[0] == 1, "single-layer trunk"
    x2d = x.reshape(batch * seq, d)

    mod, lam = _ada(c, w_ada[0], b_ada[0], lambda_q1[0], lambda_k1[0], lambda_q2[0], lambda_k2[0])
    mod3 = mod.reshape(batch, 6, d)

    proj = _in_proj(x2d, mod3, norm1_g[0], w_in[0].astype(BF16), seq)
    attn = _attention(proj, lam, subln_g[0], batch, seq)

    bs_full = jnp.broadcast_to(sgu_b_s[0][:, :, None], (N_SGU_GROUPS, SGU_CHUNK, SGU_CHUNK))
    wr_hi, wr_lo, br = _router_weights(w_router_group[0], b_router_group[0],
                                       w_router_expert[0], b_router_expert[0])
    x1, h2, logits = _mix(attn, proj, x2d, mod3, w_attn_proj[0].astype(BF16),
                          sgu_ln_g[0].reshape(1, d), sgu_ln_b[0].reshape(1, d),
                          sgu_w_s[0].astype(BF16), bs_full, w_sgu_proj[0].astype(BF16),
                          w_out[0].astype(BF16), norm2_g[0].reshape(1, d), wr_hi, wr_lo, br, seq)
    comb = _route(logits)
    out = _moe(h2, comb, w_expert_gate_up[0].astype(BF16), w_expert_down[0].astype(BF16),
               x1, mod3, final_g, seq)
    return out.reshape(batch, seq, d)
```

```python
import functools
import math

import jax
import jax.numpy as jnp
import numpy as np
from jax import lax
from jax.experimental import pallas as pl
from jax.experimental.pallas import tpu as pltpu

D_MODEL = 1024
N_HEADS = 8
HEAD_DIM = 64
HEAD_WIDTH = 2 * HEAD_DIM
N_SGU_GROUPS = 8
SGU_CHUNK = 128
N_GROUPS = 4
EXPERTS_PER_GROUP = 8
N_EXPERTS = N_GROUPS * EXPERTS_PER_GROUP
D_EXPERT = 256
N_SEGMENTS = 7
EPS = 1e-6
LAMBDA_INIT = 0.8 - 0.6 * math.exp(-0.3 * 0)
ALIBI_SLOPES = np.array([2.0 ** (-8.0 * (h + 1) / N_HEADS) for h in range(N_HEADS)], dtype=np.float32)
ROUTER_LANES = 128
GROUP_ID_LANE = 64
MOE_TILE = 1024
MOE_CHUNK = 128
MOE_EXPERTS_PER_STEP = 4
LOG2_E = math.log2(math.e)
ATTN_ROWS = 256
ATTN_KEY_TILE = 256
ATTN_L_MIN = 2.0 ** -60
ATTN_L_MAX = 2.0 ** 100

F32 = jnp.float32
BF16 = jnp.bfloat16
VMEM_LIMIT_BYTES = 56 * 1024 * 1024


def _params(semantics):
    return pltpu.CompilerParams(dimension_semantics=semantics, vmem_limit_bytes=VMEM_LIMIT_BYTES)


def _dot(a, b):
    return jnp.dot(a, b, preferred_element_type=F32)


def _sigmoid(x):
    return 1.0 / (1.0 + jnp.exp(-x))


def _gelu_tanh(x):
    return 0.5 * x * (1.0 + jnp.tanh(math.sqrt(2.0 / math.pi) * (x + 0.044715 * (x * x * x))))


def _ada_kernel(c_ref, w_ref, b_ref, lq1_ref, lk1_ref, lq2_ref, lk2_ref, mod_ref, lam_ref):
    c = c_ref[...]
    act = c * _sigmoid(c)
    mod_ref[...] = jnp.dot(act, w_ref[...], preferred_element_type=F32,
                           precision=lax.Precision.HIGHEST) + b_ref[...]
    d1 = jnp.sum(lq1_ref[...] * lk1_ref[...], axis=-1, keepdims=True)
    d2 = jnp.sum(lq2_ref[...] * lk2_ref[...], axis=-1, keepdims=True)
    lam_ref[...] = jnp.exp(d1) - jnp.exp(d2) + LAMBDA_INIT


def _ada(c, w_ada, b_ada, lq1, lk1, lq2, lk2):
    batch, d = c.shape
    n = w_ada.shape[1]
    tn = 1024
    vec = pl.BlockSpec((1, HEAD_DIM), lambda j: (0, 0))
    return pl.pallas_call(
        _ada_kernel,
        grid=(n // tn,),
        in_specs=[pl.BlockSpec((batch, d), lambda j: (0, 0)),
                  pl.BlockSpec((d, tn), lambda j: (0, j)),
                  pl.BlockSpec((1, tn), lambda j: (0, j)),
                  vec, vec, vec, vec],
        out_specs=[pl.BlockSpec((batch, tn), lambda j: (0, j)),
                   pl.BlockSpec((1, 1), lambda j: (0, 0))],
        out_shape=[jax.ShapeDtypeStruct((batch, n), F32), jax.ShapeDtypeStruct((1, 1), F32)],
        compiler_params=_params(("arbitrary",)),
        name="ada",
    )(c, w_ada, b_ada.reshape(1, n), lq1.reshape(1, -1), lk1.reshape(1, -1),
      lq2.reshape(1, -1), lk2.reshape(1, -1))


def _in_proj_kernel(x_ref, mod_ref, g_ref, w_ref, o_ref, h_scr):
    j = pl.program_id(1)

    @pl.when(j == 0)
    def _():
        x = x_ref[...]
        y = x * lax.rsqrt(jnp.mean(x * x, axis=-1, keepdims=True) + EPS) * g_ref[...]
        h_scr[...] = (y * (1.0 + mod_ref[0, 1:2, :]) + mod_ref[0, 0:1, :]).astype(BF16)

    acc = _dot(h_scr[...], w_ref[...])

    @pl.when(j == 0)
    def _():
        o_ref[0] = (acc * (HEAD_DIM ** -0.5 * LOG2_E)).astype(BF16)

    @pl.when((j == 1) | (j == 2))
    def _():
        o_ref[0] = acc.astype(BF16)

    @pl.when((j == 3) | (j == 4))
    def _():
        o_ref[0] = _gelu_tanh(acc).astype(BF16)

    @pl.when(j >= 5)
    def _():
        o_ref[0] = _sigmoid(acc).astype(BF16)


def _in_proj(x2d, mod3, norm_g, w_in_bf, seq):
    t, d = x2d.shape
    tm = 1024
    tiles_per_batch = seq // tm
    return pl.pallas_call(
        _in_proj_kernel,
        grid=(t // tm, N_SEGMENTS),
        in_specs=[pl.BlockSpec((tm, d), lambda i, j: (i, 0)),
                  pl.BlockSpec((1, 6, d), lambda i, j: (i // tiles_per_batch, 0, 0)),
                  pl.BlockSpec((1, d), lambda i, j: (0, 0)),
                  pl.BlockSpec((d, d), lambda i, j: (0, j))],
        out_specs=pl.BlockSpec((1, tm, d), lambda i, j: (j, i, 0)),
        out_shape=jax.ShapeDtypeStruct((N_SEGMENTS, t, d), BF16),
        scratch_shapes=[pltpu.VMEM((tm, d), BF16)],
        compiler_params=_params(("arbitrary", "arbitrary")),
        name="in_proj",
    )(x2d, mod3, norm_g.reshape(1, d), w_in_bf)


def _attn_epilogue(o0, o1, lam, subg):
    o = o0 - lam * o1
    o = o * lax.rsqrt(jnp.mean(o * o, axis=-1, keepdims=True) + EPS)
    return (o * subg * (1.0 - LAMBDA_INIT)).astype(BF16)


def _attn_kernel(slopes_ref, lam_ref, q_ref, k_ref, v_ref, subg_ref, o_ref,
                 bias_scr, p_scr, vext_scr, *, rows, seq):
    h = pl.program_id(0)
    b = pl.program_id(1)
    nq = seq // rows
    contract_last = (((1,), (1,)), ((), ()))

    @pl.when(b == 0)
    def _():
        slope = -LOG2_E * slopes_ref[h]
        for qb in range(nq):
            qpos = qb * rows + lax.broadcasted_iota(jnp.int32, (rows, seq), 0)
            kpos = lax.broadcasted_iota(jnp.int32, (rows, seq), 1)
            bias_scr[qb] = slope * jnp.abs(qpos - kpos).astype(F32)
        lane = lax.broadcasted_iota(jnp.int32, (seq, HEAD_WIDTH), 1)
        vext_scr[:, HEAD_WIDTH:] = jnp.where(lane == 0, 1.0, 0.0).astype(BF16)

    vext_scr[:, :HEAD_WIDTH] = v_ref[0]
    lam = lam_ref[0, 0]
    subg = subg_ref[...]

    def masked_q(r0):
        q = q_ref[0, pl.ds(r0, rows), :]
        lane = lax.broadcasted_iota(jnp.int32, q.shape, 1)
        zero = jnp.zeros_like(q)
        return jnp.where(lane < HEAD_DIM, q, zero), jnp.where(lane >= HEAD_DIM, q, zero)

    def fast_block(qb, n_bad):
        r0 = pl.multiple_of(qb * rows, rows)
        q0, q1 = masked_q(r0)
        qq = jnp.concatenate([q0, q1], axis=0)
        for c in range(seq // ATTN_KEY_TILE):
            ks = slice(c * ATTN_KEY_TILE, (c + 1) * ATTN_KEY_TILE)
            s = lax.dot_general(qq, k_ref[0, ks, :], contract_last, preferred_element_type=F32)
            bias = bias_scr[qb, :, ks]
            p_scr[:rows, ks] = jnp.exp2(s[:rows] + bias).astype(BF16)
            p_scr[rows:, ks] = jnp.exp2(s[rows:] + bias).astype(BF16)
        oo = _dot(p_scr[...], vext_scr[...])
        l0 = oo[:rows, HEAD_WIDTH:HEAD_WIDTH + 1]
        l1 = oo[rows:, HEAD_WIDTH:HEAD_WIDTH + 1]
        o_ref[pl.ds(r0, rows), :] = _attn_epilogue(oo[:rows, :HEAD_WIDTH] / l0, oo[rows:, :HEAD_WIDTH] / l1,
                                                   lam, subg)
        ok = ((l0 >= ATTN_L_MIN) & (l0 <= ATTN_L_MAX)) & ((l1 >= ATTN_L_MIN) & (l1 <= ATTN_L_MAX))
        return n_bad + jnp.where(ok, 0.0, 1.0)

    n_bad = lax.fori_loop(0, nq, fast_block, jnp.zeros((rows, 1), F32), unroll=4)

    @pl.when(jnp.sum(n_bad) > 0.0)
    def _():
        def safe_block(qb, carry):
            r0 = pl.multiple_of(qb * rows, rows)
            outs = []
            for qm in masked_q(r0):
                s = lax.dot_general(qm, k_ref[0], contract_last, preferred_element_type=F32) + bias_scr[qb]
                p = jnp.exp2(s - jnp.max(s, axis=-1, keepdims=True))
                outs.append(_dot(p.astype(BF16), v_ref[0]) / jnp.sum(p, axis=-1, keepdims=True))
            o_ref[pl.ds(r0, rows), :] = _attn_epilogue(outs[0], outs[1], lam, subg)
            return carry

        lax.fori_loop(0, nq, safe_block, 0)


def _attention(proj, lam, subln_g, batch, seq):
    t = batch * seq
    rows = ATTN_ROWS
    kernel = functools.partial(_attn_kernel, rows=rows, seq=seq)
    smem = pl.BlockSpec(memory_space=pltpu.SMEM)
    seg = lambda k: pl.BlockSpec((1, seq, HEAD_WIDTH), lambda h, b, k=k: (k, b, h))
    return pl.pallas_call(
        kernel,
        grid=(N_HEADS, batch),
        in_specs=[smem, smem, seg(0), seg(1), seg(2),
                  pl.BlockSpec((1, HEAD_WIDTH), lambda h, b: (0, 0))],
        out_specs=pl.BlockSpec((seq, HEAD_WIDTH), lambda h, b: (b, h)),
        out_shape=jax.ShapeDtypeStruct((t, N_HEADS * HEAD_WIDTH), BF16),
        scratch_shapes=[pltpu.VMEM((seq // rows, rows, seq), F32),
                        pltpu.VMEM((2 * rows, seq), BF16),
                        pltpu.VMEM((seq, 2 * HEAD_WIDTH), BF16)],
        compiler_params=_params(("arbitrary", "arbitrary")),
        name="diff_attn",
    )(jnp.asarray(ALIBI_SLOPES), lam, proj, proj, proj, subln_g.reshape(1, HEAD_WIDTH))


def _mix_kernel(attn_ref, u_ref, s_ref, ga_ref, gb_ref, x_ref, mod_ref,
                wap_ref, lng_ref, lnb_ref, ws_ref, bs_ref, wsp_ref, wout_ref,
                n2g_ref, wr_hi_ref, wr_lo_ref, br_ref,
                x1_ref, h2_ref, logit_ref, gated_scr):
    tm = x_ref.shape[0]
    y_attn = _dot(attn_ref[...], wap_ref[...])

    s = s_ref[0].astype(F32)
    mu = jnp.mean(s, axis=-1, keepdims=True)
    sc = s - mu
    var = jnp.mean(sc * sc, axis=-1, keepdims=True)
    v = ((sc * lax.rsqrt(var + EPS)) * lng_ref[...] + lnb_ref[...]).astype(BF16)
    for c in range(tm // SGU_CHUNK):
        rows = slice(c * SGU_CHUNK, (c + 1) * SGU_CHUNK)
        for g in range(N_SGU_GROUPS):
            cols = slice(g * SGU_CHUNK, (g + 1) * SGU_CHUNK)
            mixed = _dot(ws_ref[g], v[rows, cols]) + bs_ref[g]
            gated_scr[rows, cols] = (u_ref[0, rows, cols].astype(F32) * mixed).astype(BF16)
    y_sgu = _dot(gated_scr[...], wsp_ref[...])

    y = ga_ref[0].astype(F32) * y_attn + gb_ref[0].astype(F32) * y_sgu
    x1 = x_ref[...] + mod_ref[0, 2:3, :] * _dot(y.astype(BF16), wout_ref[...])
    x1_ref[...] = x1

    h2 = x1 * lax.rsqrt(jnp.mean(x1 * x1, axis=-1, keepdims=True) + EPS) * n2g_ref[...]
    h2 = h2 * (1.0 + mod_ref[0, 4:5, :]) + mod_ref[0, 3:4, :]
    h2_hi = h2.astype(BF16)
    h2_ref[...] = h2_hi
    h2_lo = (h2 - h2_hi.astype(F32)).astype(BF16)
    logit_ref[...] = (_dot(h2_hi, wr_hi_ref[...]) + _dot(h2_lo, wr_hi_ref[...])
                      + _dot(h2_hi, wr_lo_ref[...]) + br_ref[...])


def _mix(attn, proj, x2d, mod3, wap, lng, lnb, ws, bs_full, wsp, wout, n2g, wr_hi, wr_lo, br, seq):
    t, d = x2d.shape
    tm = 256
    tiles_per_batch = seq // tm
    const2 = lambda shape: pl.BlockSpec(shape, lambda i: (0, 0))
    const3 = lambda shape: pl.BlockSpec(shape, lambda i: (0, 0, 0))
    seg = lambda k: pl.BlockSpec((1, tm, d), lambda i, k=k: (k, i, 0))
    row = pl.BlockSpec((tm, d), lambda i: (i, 0))
    return pl.pallas_call(
        _mix_kernel,
        grid=(t // tm,),
        in_specs=[row, seg(3), seg(4), seg(5), seg(6), row,
                  pl.BlockSpec((1, 6, d), lambda i: (i // tiles_per_batch, 0, 0)),
                  const2((d, d)), const2((1, d)), const2((1, d)),
                  const3((N_SGU_GROUPS, SGU_CHUNK, SGU_CHUNK)),
                  const3((N_SGU_GROUPS, SGU_CHUNK, SGU_CHUNK)),
                  const2((d, d)), const2((d, d)), const2((1, d)),
                  const2((d, ROUTER_LANES)), const2((d, ROUTER_LANES)), const2((1, ROUTER_LANES))],
        out_specs=[row, row, pl.BlockSpec((tm, ROUTER_LANES), lambda i: (i, 0))],
        out_shape=[jax.ShapeDtypeStruct((t, d), F32), jax.ShapeDtypeStruct((t, d), BF16),
                   jax.ShapeDtypeStruct((t, ROUTER_LANES), F32)],
        scratch_shapes=[pltpu.VMEM((tm, d), BF16)],
        compiler_params=_params(("arbitrary",)),
        name="mix",
    )(attn, proj, proj, proj, proj, x2d, mod3, wap, lng, lnb, ws, bs_full, wsp, wout, n2g, wr_hi, wr_lo, br)


def _route_kernel(logit_ref, comb_ref, combt_ref):
    z = logit_ref[...]
    lane = lax.broadcasted_iota(jnp.int32, z.shape, 1)
    neg = jnp.float32(-jnp.inf)
    big = jnp.int32(ROUTER_LANES)

    def first_argmax(val, vmax):
        return jnp.min(jnp.where(val == vmax, lane, big), axis=-1, keepdims=True)

    is_group = (lane >= N_EXPERTS) & (lane < N_EXPERTS + N_GROUPS)
    gl = jnp.where(is_group, z, neg)
    gmax = jnp.max(gl, axis=-1, keepdims=True)
    ge = jnp.exp(gl - gmax)
    gp = ge / jnp.sum(ge, axis=-1, keepdims=True)
    gval = jnp.max(gp, axis=-1, keepdims=True)
    gidx = first_argmax(gp, gval) - N_EXPERTS

    in_group = (lane >= gidx * EXPERTS_PER_GROUP) & (lane < (gidx + 1) * EXPERTS_PER_GROUP)
    el = jnp.where(in_group, z, neg)
    emax = jnp.max(el, axis=-1, keepdims=True)
    ee = jnp.exp(el - emax)
    ep = ee / jnp.sum(ee, axis=-1, keepdims=True)
    ep = jnp.where(in_group, ep, -1.0)
    ev0 = jnp.max(ep, axis=-1, keepdims=True)
    ei0 = first_argmax(ep, ev0)
    ep_rest = jnp.where(lane == ei0, -1.0, ep)
    ev1 = jnp.max(ep_rest, axis=-1, keepdims=True)
    ei1 = first_argmax(ep_rest, ev1)
    denom = ev0 + ev1
    comb = jnp.where(lane == ei0, ev0 / denom * gval,
                     jnp.where(lane == ei1, ev1 / denom * gval,
                               jnp.where(lane == GROUP_ID_LANE, gidx.astype(F32), 0.0)))
    comb_ref[...] = comb
    combt_ref[...] = comb.T


def _route(logits):
    t = logits.shape[0]
    tm = 1024
    spec = pl.BlockSpec((tm, ROUTER_LANES), lambda i: (i, 0))
    return pl.pallas_call(
        _route_kernel, grid=(t // tm,), in_specs=[spec],
        out_specs=[spec, pl.BlockSpec((ROUTER_LANES, tm), lambda i: (0, i))],
        out_shape=[jax.ShapeDtypeStruct((t, ROUTER_LANES), F32), jax.ShapeDtypeStruct((ROUTER_LANES, t), F32)],
        compiler_params=_params(("arbitrary",)), name="route",
    )(logits)


def _moe_kernel(h2_ref, comb_ref, combt_ref, wgu_ref, wd_ref, x1_ref, mod_ref, fg_ref, out_ref,
                onehot_scr, xs_scr, wl_scr, outc_scr, meta_ref):
    j = pl.program_id(1)
    tt = h2_ref.shape[0]
    rc = onehot_scr.shape[0]
    steps_per_group = EXPERTS_PER_GROUP // MOE_EXPERTS_PER_STEP
    g = j // steps_per_group

    @pl.when(j == 0)
    def _():
        gid = combt_ref[GROUP_ID_LANE:GROUP_ID_LANE + 1, :]
        sub = lax.broadcasted_iota(jnp.int32, (8, tt), 0).astype(F32)
        member = jnp.where(gid == sub, 1.0, 0.0)
        before = jnp.where(lax.broadcasted_iota(jnp.int32, (tt, tt), 0)
                           < lax.broadcasted_iota(jnp.int32, (tt, tt), 1), 1.0, 0.0).astype(BF16)
        rank = _dot(member.astype(BF16), before)
        pos = jnp.zeros((1, tt), F32)
        off = jnp.int32(0)
        for gg in range(N_GROUPS):
            count = jnp.sum(member[gg:gg + 1, :]).astype(jnp.int32)
            n_chunks = (count + (MOE_CHUNK - 1)) // MOE_CHUNK
            meta_ref[gg] = off
            meta_ref[N_GROUPS + gg] = n_chunks
            pos = pos + member[gg:gg + 1, :] * (off.astype(F32) + rank[gg:gg + 1, :])
            off = off + n_chunks * MOE_CHUNK
        row = lax.broadcasted_iota(jnp.int32, (rc, tt), 0).astype(F32)
        onehot = jnp.where(row == pos, 1.0, 0.0).astype(BF16)
        onehot_scr[...] = onehot
        xs_scr[...] = _dot(onehot, h2_ref[...]).astype(BF16)
        comb = comb_ref[...]
        c_hi = comb.astype(BF16)
        c_lo = (comb - c_hi.astype(F32)).astype(BF16)
        wl2 = _dot(onehot, jnp.concatenate([c_hi, c_lo], axis=1))
        wl_scr[...] = wl2[:, :ROUTER_LANES] + wl2[:, ROUTER_LANES:]
        outc_scr[...] = jnp.zeros_like(outc_scr)

    off_g = meta_ref[g]
    n_chunks_g = meta_ref[N_GROUPS + g]

    def chunk(k, carry):
        r0 = pl.multiple_of(off_g + k * MOE_CHUNK, MOE_CHUNK)
        xs = xs_scr[pl.ds(r0, MOE_CHUNK), :]
        wl = wl_scr[pl.ds(r0, MOE_CHUNK), :]
        lane = lax.broadcasted_iota(jnp.int32, wl.shape, 1)
        acts = []
        for e in range(MOE_EXPERTS_PER_STEP):
            gu = _dot(xs, wgu_ref[e])
            gate = gu[:, :D_EXPERT]
            up = gu[:, D_EXPERT:]
            w = jnp.sum(jnp.where(lane == j * MOE_EXPERTS_PER_STEP + e, wl, 0.0), axis=-1, keepdims=True)
            acts.append((gate * _sigmoid(gate) * up * w).astype(BF16))
        wd_all = wd_ref[...].reshape(MOE_EXPERTS_PER_STEP * D_EXPERT, D_MODEL)
        outc_scr[pl.ds(r0, MOE_CHUNK), :] += _dot(jnp.concatenate(acts, axis=1), wd_all)
        return carry

    lax.fori_loop(0, n_chunks_g, chunk, 0)

    @pl.when(j == pl.num_programs(1) - 1)
    def _():
        y = lax.dot_general(onehot_scr[...], outc_scr[...].astype(BF16), (((0,), (0,)), ((), ())),
                            preferred_element_type=F32)
        x2 = x1_ref[...] + mod_ref[0, 5:6, :] * y
        out_ref[...] = x2 * lax.rsqrt(jnp.mean(x2 * x2, axis=-1, keepdims=True) + EPS) * fg_ref[...]


def _moe(h2, comb, combt, wgu, wd, x1, mod3, final_g, seq):
    t, d = x1.shape
    tm = MOE_TILE
    tiles_per_batch = seq // tm
    n_steps = N_EXPERTS // MOE_EXPERTS_PER_STEP
    compact_rows = tm + N_GROUPS * MOE_CHUNK
    row = pl.BlockSpec((tm, d), lambda i, j: (i, 0))
    return pl.pallas_call(
        _moe_kernel,
        grid=(t // tm, n_steps),
        in_specs=[row,
                  pl.BlockSpec((tm, ROUTER_LANES), lambda i, j: (i, 0)),
                  pl.BlockSpec((ROUTER_LANES, tm), lambda i, j: (0, i)),
                  pl.BlockSpec((MOE_EXPERTS_PER_STEP, d, 2 * D_EXPERT), lambda i, j: (j, 0, 0)),
                  pl.BlockSpec((MOE_EXPERTS_PER_STEP, D_EXPERT, d), lambda i, j: (j, 0, 0)),
                  pl.BlockSpec((tm, d), lambda i, j: (i, 0), pipeline_mode=pl.Buffered(1)),
                  pl.BlockSpec((1, 6, d), lambda i, j: (i // tiles_per_batch, 0, 0)),
                  pl.BlockSpec((1, d), lambda i, j: (0, 0))],
        out_specs=row,
        out_shape=jax.ShapeDtypeStruct((t, d), F32),
        scratch_shapes=[pltpu.VMEM((compact_rows, tm), BF16),
                        pltpu.VMEM((compact_rows, d), BF16),
                        pltpu.VMEM((compact_rows, ROUTER_LANES), F32),
                        pltpu.VMEM((compact_rows, d), F32),
                        pltpu.SMEM((2 * N_GROUPS,), jnp.int32)],
        compiler_params=_params(("arbitrary", "arbitrary")),
        name="moe",
    )(h2, comb, combt, wgu, wd, x1, mod3, final_g.reshape(1, d))


def _router_weights(w_rg, b_rg, w_re, b_re):
    d = w_rg.shape[0]
    pad = ROUTER_LANES - N_EXPERTS - N_GROUPS
    w = jnp.concatenate([w_re, w_rg, jnp.zeros((d, pad), F32)], axis=1)
    b = jnp.concatenate([b_re, b_rg, jnp.zeros((pad,), F32)]).reshape(1, ROUTER_LANES)
    w_hi = w.astype(BF16)
    w_lo = (w - w_hi.astype(F32)).astype(BF16)
    return w_hi, w_lo, b


def kernel(x, c, w_ada, b_ada, norm1_g, w_in, lambda_q1, lambda_k1, lambda_q2, lambda_k2, subln_g, w_attn_proj, sgu_ln_g, sgu_ln_b, sgu_w_s, sgu_b_s, w_sgu_proj, w_out, norm2_g, w_router_group, b_router_group, w_router_expert, b_router_expert, w_expert_gate_up, w_expert_down, final_g):
    batch, seq, d = x.shape
    assert w_ada.shape[0] == 1, "single-layer trunk"
    x2d = x.reshape(batch * seq, d)

    mod, lam = _ada(c, w_ada[0], b_ada[0], lambda_q1[0], lambda_k1[0], lambda_q2[0], lambda_k2[0])
    mod3 = mod.reshape(batch, 6, d)

    proj = _in_proj(x2d, mod3, norm1_g[0], w_in[0].astype(BF16), seq)
    attn = _attention(proj, lam, subln_g[0], batch, seq)

    bs_full = jnp.broadcast_to(sgu_b_s[0][:, :, None], (N_SGU_GROUPS, SGU_CHUNK, SGU_CHUNK))
    wr_hi, wr_lo, br = _router_weights(w_router_group[0], b_router_group[0],
                                       w_router_expert[0], b_router_expert[0])
    x1, h2, logits = _mix(attn, proj, x2d, mod3, w_attn_proj[0].astype(BF16),
                          sgu_ln_g[0].reshape(1, d), sgu_ln_b[0].reshape(1, d),
                          sgu_w_s[0].astype(BF16), bs_full, w_sgu_proj[0].astype(BF16),
                          w_out[0].astype(BF16), norm2_g[0].reshape(1, d), wr_hi, wr_lo, br, seq)
    comb, combt = _route(logits)
    out = _moe(h2, comb, combt, w_expert_gate_up[0].astype(BF16), w_expert_down[0].astype(BF16),
               x1, mod3, final_g, seq)
    return out.reshape(batch, seq, d)
```

```python
import functools
import math

import jax
import jax.numpy as jnp
import numpy as np
from jax import lax
from jax.experimental import pallas as pl
from jax.experimental.pallas import tpu as pltpu

D_MODEL = 1024
N_HEADS = 8
HEAD_DIM = 64
HEAD_WIDTH = 2 * HEAD_DIM
N_SGU_GROUPS = 8
SGU_CHUNK = 128
N_GROUPS = 4
EXPERTS_PER_GROUP = 8
N_EXPERTS = N_GROUPS * EXPERTS_PER_GROUP
D_EXPERT = 256
N_SEGMENTS = 7
EPS = 1e-6
LAMBDA_INIT = 0.8 - 0.6 * math.exp(-0.3 * 0)
ALIBI_SLOPES = np.array([2.0 ** (-8.0 * (h + 1) / N_HEADS) for h in range(N_HEADS)], dtype=np.float32)
ROUTER_LANES = 128
GROUP_ID_LANE = 64
MOE_TILE = 1024
MOE_CHUNK = 128
MOE_EXPERTS_PER_STEP = 4
LOG2_E = math.log2(math.e)
IN_PROJ_SUB = 256
MIX_TILE = 512
ATTN_ROWS = 256
ATTN_KEY_TILE = 256
ATTN_L_MIN = 2.0 ** -60
ATTN_L_MAX = 2.0 ** 100

F32 = jnp.float32
BF16 = jnp.bfloat16
VMEM_LIMIT_BYTES = 56 * 1024 * 1024


def _params(semantics):
    return pltpu.CompilerParams(dimension_semantics=semantics, vmem_limit_bytes=VMEM_LIMIT_BYTES)


def _dot(a, b):
    return jnp.dot(a, b, preferred_element_type=F32)


def _sigmoid(x):
    return 1.0 / (1.0 + jnp.exp(-x))


def _gelu_tanh(x):
    return 0.5 * x * (1.0 + jnp.tanh(math.sqrt(2.0 / math.pi) * (x + 0.044715 * (x * x * x))))


def _ada_kernel(c_ref, w_ref, b_ref, lq1_ref, lk1_ref, lq2_ref, lk2_ref, mod_ref, lam_ref):
    c = c_ref[...]
    act = c * _sigmoid(c)
    mod_ref[...] = jnp.dot(act, w_ref[...], preferred_element_type=F32,
                           precision=lax.Precision.HIGHEST) + b_ref[...]
    d1 = jnp.sum(lq1_ref[...] * lk1_ref[...], axis=-1, keepdims=True)
    d2 = jnp.sum(lq2_ref[...] * lk2_ref[...], axis=-1, keepdims=True)
    lam_ref[...] = jnp.exp(d1) - jnp.exp(d2) + LAMBDA_INIT


def _ada(c, w_ada, b_ada, lq1, lk1, lq2, lk2):
    batch, d = c.shape
    n = w_ada.shape[1]
    tn = 1024
    vec = pl.BlockSpec((1, HEAD_DIM), lambda j: (0, 0))
    return pl.pallas_call(
        _ada_kernel,
        grid=(n // tn,),
        in_specs=[pl.BlockSpec((batch, d), lambda j: (0, 0)),
                  pl.BlockSpec((d, tn), lambda j: (0, j)),
                  pl.BlockSpec((1, tn), lambda j: (0, j)),
                  vec, vec, vec, vec],
        out_specs=[pl.BlockSpec((batch, tn), lambda j: (0, j)),
                   pl.BlockSpec((1, 1), lambda j: (0, 0))],
        out_shape=[jax.ShapeDtypeStruct((batch, n), F32), jax.ShapeDtypeStruct((1, 1), F32)],
        compiler_params=_params(("arbitrary",)),
        name="ada",
    )(c, w_ada, b_ada.reshape(1, n), lq1.reshape(1, -1), lk1.reshape(1, -1),
      lq2.reshape(1, -1), lk2.reshape(1, -1))


def _in_proj_kernel(x_ref, mod_ref, g_ref, w_ref, o_ref, h_scr):
    j = pl.program_id(1)

    @pl.when(j == 0)
    def _():
        x = x_ref[...]
        y = x * lax.rsqrt(jnp.mean(x * x, axis=-1, keepdims=True) + EPS) * g_ref[...]
        h_scr[...] = (y * (1.0 + mod_ref[0, 1:2, :]) + mod_ref[0, 0:1, :]).astype(BF16)

    def project(epilogue):
        for n in range(w_ref.shape[1] // IN_PROJ_SUB):
            cs = slice(n * IN_PROJ_SUB, (n + 1) * IN_PROJ_SUB)
            o_ref[0, :, cs] = epilogue(_dot(h_scr[...], w_ref[:, cs])).astype(BF16)

    @pl.when(j == 0)
    def _():
        project(lambda acc: acc * (HEAD_DIM ** -0.5 * LOG2_E))

    @pl.when((j == 1) | (j == 2))
    def _():
        project(lambda acc: acc)

    @pl.when((j == 3) | (j == 4))
    def _():
        project(_gelu_tanh)

    @pl.when(j >= 5)
    def _():
        project(_sigmoid)


def _in_proj(x2d, mod3, norm_g, w_in_bf, seq):
    t, d = x2d.shape
    tm = 1024
    tiles_per_batch = seq // tm
    return pl.pallas_call(
        _in_proj_kernel,
        grid=(t // tm, N_SEGMENTS),
        in_specs=[pl.BlockSpec((tm, d), lambda i, j: (i, 0)),
                  pl.BlockSpec((1, 6, d), lambda i, j: (i // tiles_per_batch, 0, 0)),
                  pl.BlockSpec((1, d), lambda i, j: (0, 0)),
                  pl.BlockSpec((d, d), lambda i, j: (0, j))],
        out_specs=pl.BlockSpec((1, tm, d), lambda i, j: (j, i, 0)),
        out_shape=jax.ShapeDtypeStruct((N_SEGMENTS, t, d), BF16),
        scratch_shapes=[pltpu.VMEM((tm, d), BF16)],
        compiler_params=_params(("arbitrary", "arbitrary")),
        name="in_proj",
    )(x2d, mod3, norm_g.reshape(1, d), w_in_bf)


def _attn_epilogue(o0, o1, lam, subg):
    o = o0 - lam * o1
    o = o * lax.rsqrt(jnp.mean(o * o, axis=-1, keepdims=True) + EPS)
    return (o * subg * (1.0 - LAMBDA_INIT)).astype(BF16)


def _attn_kernel(slopes_ref, lam_ref, q_ref, k_ref, v_ref, subg_ref, o_ref,
                 bias_scr, p_scr, vext_scr, *, rows, seq):
    h = pl.program_id(0)
    b = pl.program_id(1)
    nq = seq // rows
    contract_last = (((1,), (1,)), ((), ()))

    @pl.when(b == 0)
    def _():
        slope = -LOG2_E * slopes_ref[h]
        for qb in range(nq):
            qpos = qb * rows + lax.broadcasted_iota(jnp.int32, (rows, seq), 0)
            kpos = lax.broadcasted_iota(jnp.int32, (rows, seq), 1)
            bias_scr[qb] = slope * jnp.abs(qpos - kpos).astype(F32)
        lane = lax.broadcasted_iota(jnp.int32, (seq, HEAD_WIDTH), 1)
        vext_scr[:, HEAD_WIDTH:] = jnp.where(lane == 0, 1.0, 0.0).astype(BF16)

    vext_scr[:, :HEAD_WIDTH] = v_ref[0]
    lam = lam_ref[0, 0]
    subg = subg_ref[...]

    def masked_q(r0):
        q = q_ref[0, pl.ds(r0, rows), :]
        lane = lax.broadcasted_iota(jnp.int32, q.shape, 1)
        zero = jnp.zeros_like(q)
        return jnp.where(lane < HEAD_DIM, q, zero), jnp.where(lane >= HEAD_DIM, q, zero)

    def fast_block(qb, n_bad):
        r0 = pl.multiple_of(qb * rows, rows)
        q0, q1 = masked_q(r0)
        qq = jnp.concatenate([q0, q1], axis=0)
        for c in range(seq // ATTN_KEY_TILE):
            ks = slice(c * ATTN_KEY_TILE, (c + 1) * ATTN_KEY_TILE)
            s = lax.dot_general(qq, k_ref[0, ks, :], contract_last, preferred_element_type=F32)
            bias = bias_scr[qb, :, ks]
            p_scr[:rows, ks] = jnp.exp2(s[:rows] + bias).astype(BF16)
            p_scr[rows:, ks] = jnp.exp2(s[rows:] + bias).astype(BF16)
        oo = _dot(p_scr[...], vext_scr[...])
        l0 = oo[:rows, HEAD_WIDTH:HEAD_WIDTH + 1]
        l1 = oo[rows:, HEAD_WIDTH:HEAD_WIDTH + 1]
        o_ref[pl.ds(r0, rows), :] = _attn_epilogue(oo[:rows, :HEAD_WIDTH] / l0, oo[rows:, :HEAD_WIDTH] / l1,
                                                   lam, subg)
        ok = ((l0 >= ATTN_L_MIN) & (l0 <= ATTN_L_MAX)) & ((l1 >= ATTN_L_MIN) & (l1 <= ATTN_L_MAX))
        return n_bad + jnp.where(ok, 0.0, 1.0)

    n_bad = lax.fori_loop(0, nq, fast_block, jnp.zeros((rows, 1), F32), unroll=True)

    @pl.when(jnp.sum(n_bad) > 0.0)
    def _():
        def safe_block(qb, carry):
            r0 = pl.multiple_of(qb * rows, rows)
            outs = []
            for qm in masked_q(r0):
                s = lax.dot_general(qm, k_ref[0], contract_last, preferred_element_type=F32) + bias_scr[qb]
                p = jnp.exp2(s - jnp.max(s, axis=-1, keepdims=True))
                outs.append(_dot(p.astype(BF16), v_ref[0]) / jnp.sum(p, axis=-1, keepdims=True))
            o_ref[pl.ds(r0, rows), :] = _attn_epilogue(outs[0], outs[1], lam, subg)
            return carry

        lax.fori_loop(0, nq, safe_block, 0)


def _attention(proj, lam, subln_g, batch, seq):
    t = batch * seq
    rows = ATTN_ROWS
    kernel = functools.partial(_attn_kernel, rows=rows, seq=seq)
    smem = pl.BlockSpec(memory_space=pltpu.SMEM)
    seg = lambda k: pl.BlockSpec((1, seq, HEAD_WIDTH), lambda h, b, k=k: (k, b, h))
    return pl.pallas_call(
        kernel,
        grid=(N_HEADS, batch),
        in_specs=[smem, smem, seg(0), seg(1), seg(2),
                  pl.BlockSpec((1, HEAD_WIDTH), lambda h, b: (0, 0))],
        out_specs=pl.BlockSpec((seq, HEAD_WIDTH), lambda h, b: (b, h)),
        out_shape=jax.ShapeDtypeStruct((t, N_HEADS * HEAD_WIDTH), BF16),
        scratch_shapes=[pltpu.VMEM((seq // rows, rows, seq), F32),
                        pltpu.VMEM((2 * rows, seq), BF16),
                        pltpu.VMEM((seq, 2 * HEAD_WIDTH), BF16)],
        compiler_params=_params(("arbitrary", "arbitrary")),
        name="diff_attn",
    )(jnp.asarray(ALIBI_SLOPES), lam, proj, proj, proj, subln_g.reshape(1, HEAD_WIDTH))


def _mix_kernel(attn_ref, u_ref, s_ref, ga_ref, gb_ref, x_ref, mod_ref,
                wap_ref, lng_ref, lnb_ref, ws_ref, bs_ref, wsp_ref, wout_ref,
                n2g_ref, wr_ref, br_ref,
                x1_ref, h2_ref, logit_ref, gated_scr):
    tm = x_ref.shape[0]
    y_attn = _dot(attn_ref[...], wap_ref[...])

    s = s_ref[0].astype(F32)
    mu = jnp.mean(s, axis=-1, keepdims=True)
    sc = s - mu
    var = jnp.mean(sc * sc, axis=-1, keepdims=True)
    v = ((sc * lax.rsqrt(var + EPS)) * lng_ref[...] + lnb_ref[...]).astype(BF16)
    n_chunks = tm // SGU_CHUNK
    for g in range(N_SGU_GROUPS):
        cols = slice(g * SGU_CHUNK, (g + 1) * SGU_CHUNK)
        v_g = jnp.concatenate([v[c * SGU_CHUNK:(c + 1) * SGU_CHUNK, cols] for c in range(n_chunks)], axis=1)
        mixed_g = _dot(ws_ref[g], v_g)
        for c in range(n_chunks):
            rows = slice(c * SGU_CHUNK, (c + 1) * SGU_CHUNK)
            mixed = mixed_g[:, c * SGU_CHUNK:(c + 1) * SGU_CHUNK] + bs_ref[g]
            gated_scr[rows, cols] = (u_ref[0, rows, cols].astype(F32) * mixed).astype(BF16)
    y_sgu = _dot(gated_scr[...], wsp_ref[...])

    y = ga_ref[0].astype(F32) * y_attn + gb_ref[0].astype(F32) * y_sgu
    x1 = x_ref[...] + mod_ref[0, 2:3, :] * _dot(y.astype(BF16), wout_ref[...])
    x1_ref[...] = x1

    h2 = x1 * lax.rsqrt(jnp.mean(x1 * x1, axis=-1, keepdims=True) + EPS) * n2g_ref[...]
    h2 = h2 * (1.0 + mod_ref[0, 4:5, :]) + mod_ref[0, 3:4, :]
    h2_hi = h2.astype(BF16)
    h2_ref[...] = h2_hi
    h2_lo = (h2 - h2_hi.astype(F32)).astype(BF16)
    parts = _dot(jnp.concatenate([h2_hi, h2_lo], axis=0), wr_ref[...])
    logit_ref[...] = ((parts[:tm, :ROUTER_LANES] + parts[:tm, ROUTER_LANES:])
                      + (parts[tm:, :ROUTER_LANES] + parts[tm:, ROUTER_LANES:]) + br_ref[...])


def _mix(attn, proj, x2d, mod3, wap, lng, lnb, ws, bs_full, wsp, wout, n2g, wr, br, seq):
    t, d = x2d.shape
    tm = MIX_TILE
    tiles_per_batch = seq // tm
    const2 = lambda shape: pl.BlockSpec(shape, lambda i: (0, 0))
    const3 = lambda shape: pl.BlockSpec(shape, lambda i: (0, 0, 0))
    seg = lambda k: pl.BlockSpec((1, tm, d), lambda i, k=k: (k, i, 0))
    row = pl.BlockSpec((tm, d), lambda i: (i, 0))
    return pl.pallas_call(
        _mix_kernel,
        grid=(t // tm,),
        in_specs=[row, seg(3), seg(4), seg(5), seg(6), row,
                  pl.BlockSpec((1, 6, d), lambda i: (i // tiles_per_batch, 0, 0)),
                  const2((d, d)), const2((1, d)), const2((1, d)),
                  const3((N_SGU_GROUPS, SGU_CHUNK, SGU_CHUNK)),
                  const3((N_SGU_GROUPS, SGU_CHUNK, SGU_CHUNK)),
                  const2((d, d)), const2((d, d)), const2((1, d)),
                  const2((d, 2 * ROUTER_LANES)), const2((1, ROUTER_LANES))],
        out_specs=[row, row, pl.BlockSpec((tm, ROUTER_LANES), lambda i: (i, 0))],
        out_shape=[jax.ShapeDtypeStruct((t, d), F32), jax.ShapeDtypeStruct((t, d), BF16),
                   jax.ShapeDtypeStruct((t, ROUTER_LANES), F32)],
        scratch_shapes=[pltpu.VMEM((tm, d), BF16)],
        compiler_params=_params(("arbitrary",)),
        name="mix",
    )(attn, proj, proj, proj, proj, x2d, mod3, wap, lng, lnb, ws, bs_full, wsp, wout, n2g, wr, br)


def _route_kernel(logit_ref, comb_ref, combt_ref):
    z = logit_ref[...]
    lane = lax.broadcasted_iota(jnp.int32, z.shape, 1)
    neg = jnp.float32(-jnp.inf)
    big = jnp.int32(ROUTER_LANES)

    def first_argmax(val, vmax):
        return jnp.min(jnp.where(val == vmax, lane, big), axis=-1, keepdims=True)

    is_group = (lane >= N_EXPERTS) & (lane < N_EXPERTS + N_GROUPS)
    gl = jnp.where(is_group, z, neg)
    gmax = jnp.max(gl, axis=-1, keepdims=True)
    ge = jnp.exp(gl - gmax)
    gp = ge / jnp.sum(ge, axis=-1, keepdims=True)
    gval = jnp.max(gp, axis=-1, keepdims=True)
    gidx = first_argmax(gp, gval) - N_EXPERTS

    in_group = (lane >= gidx * EXPERTS_PER_GROUP) & (lane < (gidx + 1) * EXPERTS_PER_GROUP)
    el = jnp.where(in_group, z, neg)
    emax = jnp.max(el, axis=-1, keepdims=True)
    ee = jnp.exp(el - emax)
    ep = ee / jnp.sum(ee, axis=-1, keepdims=True)
    ep = jnp.where(in_group, ep, -1.0)
    ev0 = jnp.max(ep, axis=-1, keepdims=True)
    ei0 = first_argmax(ep, ev0)
    ep_rest = jnp.where(lane == ei0, -1.0, ep)
    ev1 = jnp.max(ep_rest, axis=-1, keepdims=True)
    ei1 = first_argmax(ep_rest, ev1)
    denom = ev0 + ev1
    comb = jnp.where(lane == ei0, ev0 / denom * gval,
                     jnp.where(lane == ei1, ev1 / denom * gval,
                               jnp.where(lane == GROUP_ID_LANE, gidx.astype(F32), 0.0)))
    comb_ref[...] = comb
    combt_ref[...] = comb.T


def _route(logits):
    t = logits.shape[0]
    tm = 1024
    spec = pl.BlockSpec((tm, ROUTER_LANES), lambda i: (i, 0))
    return pl.pallas_call(
        _route_kernel, grid=(t // tm,), in_specs=[spec],
        out_specs=[spec, pl.BlockSpec((ROUTER_LANES, tm), lambda i: (0, i))],
        out_shape=[jax.ShapeDtypeStruct((t, ROUTER_LANES), F32), jax.ShapeDtypeStruct((ROUTER_LANES, t), F32)],
        compiler_params=_params(("arbitrary",)), name="route",
    )(logits)


def _moe_kernel(h2_ref, comb_ref, combt_ref, wgu_ref, wd_ref, x1_ref, mod_ref, fg_ref, out_ref,
                onehot_scr, xs_scr, wl_scr, outc_scr, meta_ref):
    j = pl.program_id(1)
    tt = h2_ref.shape[0]
    rc = onehot_scr.shape[0]
    steps_per_group = EXPERTS_PER_GROUP // MOE_EXPERTS_PER_STEP
    g = j // steps_per_group

    @pl.when(j == 0)
    def _():
        gid = combt_ref[GROUP_ID_LANE:GROUP_ID_LANE + 1, :]
        sub = lax.broadcasted_iota(jnp.int32, (8, tt), 0).astype(F32)
        member = jnp.where(gid == sub, 1.0, 0.0)
        before = jnp.where(lax.broadcasted_iota(jnp.int32, (tt, tt), 0)
                           < lax.broadcasted_iota(jnp.int32, (tt, tt), 1), 1.0, 0.0).astype(BF16)
        rank = _dot(member.astype(BF16), before)
        pos = jnp.zeros((1, tt), F32)
        off = jnp.int32(0)
        for gg in range(N_GROUPS):
            count = jnp.sum(member[gg:gg + 1, :]).astype(jnp.int32)
            n_chunks = (count + (MOE_CHUNK - 1)) // MOE_CHUNK
            meta_ref[gg] = off
            meta_ref[N_GROUPS + gg] = n_chunks
            pos = pos + member[gg:gg + 1, :] * (off.astype(F32) + rank[gg:gg + 1, :])
            off = off + n_chunks * MOE_CHUNK
        row = lax.broadcasted_iota(jnp.int32, (rc, tt), 0).astype(F32)
        onehot = jnp.where(row == pos, 1.0, 0.0).astype(BF16)
        onehot_scr[...] = onehot
        xs_scr[...] = _dot(onehot, h2_ref[...]).astype(BF16)
        comb = comb_ref[...]
        c_hi = comb.astype(BF16)
        c_lo = (comb - c_hi.astype(F32)).astype(BF16)
        wl2 = _dot(onehot, jnp.concatenate([c_hi, c_lo], axis=1))
        wl_scr[...] = wl2[:, :ROUTER_LANES] + wl2[:, ROUTER_LANES:]
        outc_scr[...] = jnp.zeros_like(outc_scr)

    off_g = meta_ref[g]
    n_chunks_g = meta_ref[N_GROUPS + g]

    def chunk(k, carry):
        r0 = pl.multiple_of(off_g + k * MOE_CHUNK, MOE_CHUNK)
        xs = xs_scr[pl.ds(r0, MOE_CHUNK), :]
        wl = wl_scr[pl.ds(r0, MOE_CHUNK), :]
        lane = lax.broadcasted_iota(jnp.int32, wl.shape, 1)
        acts = []
        for e in range(MOE_EXPERTS_PER_STEP):
            gu = _dot(xs, wgu_ref[e])
            gate = gu[:, :D_EXPERT]
            up = gu[:, D_EXPERT:]
            w = jnp.sum(jnp.where(lane == j * MOE_EXPERTS_PER_STEP + e, wl, 0.0), axis=-1, keepdims=True)
            acts.append((gate * _sigmoid(gate) * up * w).astype(BF16))
        wd_all = wd_ref[...].reshape(MOE_EXPERTS_PER_STEP * D_EXPERT, D_MODEL)
        outc_scr[pl.ds(r0, MOE_CHUNK), :] += _dot(jnp.concatenate(acts, axis=1), wd_all)
        return carry

    lax.fori_loop(0, n_chunks_g, chunk, 0)

    @pl.when(j == pl.num_programs(1) - 1)
    def _():
        y = lax.dot_general(onehot_scr[...], outc_scr[...].astype(BF16), (((0,), (0,)), ((), ())),
                            preferred_element_type=F32)
        x2 = x1_ref[...] + mod_ref[0, 5:6, :] * y
        out_ref[...] = x2 * lax.rsqrt(jnp.mean(x2 * x2, axis=-1, keepdims=True) + EPS) * fg_ref[...]


def _moe(h2, comb, combt, wgu, wd, x1, mod3, final_g, seq):
    t, d = x1.shape
    tm = MOE_TILE
    tiles_per_batch = seq // tm
    n_steps = N_EXPERTS // MOE_EXPERTS_PER_STEP
    compact_rows = tm + N_GROUPS * MOE_CHUNK
    row = pl.BlockSpec((tm, d), lambda i, j: (i, 0))
    return pl.pallas_call(
        _moe_kernel,
        grid=(t // tm, n_steps),
        in_specs=[row,
                  pl.BlockSpec((tm, ROUTER_LANES), lambda i, j: (i, 0)),
                  pl.BlockSpec((ROUTER_LANES, tm), lambda i, j: (0, i)),
                  pl.BlockSpec((MOE_EXPERTS_PER_STEP, d, 2 * D_EXPERT), lambda i, j: (j, 0, 0)),
                  pl.BlockSpec((MOE_EXPERTS_PER_STEP, D_EXPERT, d), lambda i, j: (j, 0, 0)),
                  pl.BlockSpec((tm, d), lambda i, j: (i, 0), pipeline_mode=pl.Buffered(1)),
                  pl.BlockSpec((1, 6, d), lambda i, j: (i // tiles_per_batch, 0, 0)),
                  pl.BlockSpec((1, d), lambda i, j: (0, 0))],
        out_specs=row,
        out_shape=jax.ShapeDtypeStruct((t, d), F32),
        scratch_shapes=[pltpu.VMEM((compact_rows, tm), BF16),
                        pltpu.VMEM((compact_rows, d), BF16),
                        pltpu.VMEM((compact_rows, ROUTER_LANES), F32),
                        pltpu.VMEM((compact_rows, d), F32),
                        pltpu.SMEM((2 * N_GROUPS,), jnp.int32)],
        compiler_params=_params(("arbitrary", "arbitrary")),
        name="moe",
    )(h2, comb, combt, wgu, wd, x1, mod3, final_g.reshape(1, d))


def _router_weights(w_rg, b_rg, w_re, b_re):
    d = w_rg.shape[0]
    pad = ROUTER_LANES - N_EXPERTS - N_GROUPS
    w = jnp.concatenate([w_re, w_rg, jnp.zeros((d, pad), F32)], axis=1)
    b = jnp.concatenate([b_re, b_rg, jnp.zeros((pad,), F32)]).reshape(1, ROUTER_LANES)
    w_hi = w.astype(BF16)
    w_lo = (w - w_hi.astype(F32)).astype(BF16)
    return jnp.concatenate([w_hi, w_lo], axis=1), b


def kernel(x, c, w_ada, b_ada, norm1_g, w_in, lambda_q1, lambda_k1, lambda_q2, lambda_k2, subln_g, w_attn_proj, sgu_ln_g, sgu_ln_b, sgu_w_s, sgu_b_s, w_sgu_proj, w_out, norm2_g, w_router_group, b_router_group, w_router_expert, b_router_expert, w_expert_gate_up, w_expert_down, final_g):
    batch, seq, d = x.shape
    assert w_ada.shape[0] == 1, "single-layer trunk"
    x2d = x.reshape(batch * seq, d)

    mod, lam = _ada(c, w_ada[0], b_ada[0], lambda_q1[0], lambda_k1[0], lambda_q2[0], lambda_k2[0])
    mod3 = mod.reshape(batch, 6, d)

    proj = _in_proj(x2d, mod3, norm1_g[0], w_in[0].astype(BF16), seq)
    attn = _attention(proj, lam, subln_g[0], batch, seq)

    bs_full = jnp.broadcast_to(sgu_b_s[0][:, :, None], (N_SGU_GROUPS, SGU_CHUNK, SGU_CHUNK))
    wr, br = _router_weights(w_router_group[0], b_router_group[0], w_router_expert[0], b_router_expert[0])
    x1, h2, logits = _mix(attn, proj, x2d, mod3, w_attn_proj[0].astype(BF16),
                          sgu_ln_g[0].reshape(1, d), sgu_ln_b[0].reshape(1, d),
                          sgu_w_s[0].astype(BF16), bs_full, w_sgu_proj[0].astype(BF16),
                          w_out[0].astype(BF16), norm2_g[0].reshape(1, d), wr, br, seq)
    comb, combt = _route(logits)
    out = _moe(h2, comb, combt, w_expert_gate_up[0].astype(BF16), w_expert_down[0].astype(BF16),
               x1, mod3, final_g, seq)
    return out.reshape(batch, seq, d)
```

```python
import functools
import math

import jax
import jax.numpy as jnp
import numpy as np
from jax import lax
from jax.experimental import pallas as pl
from jax.experimental.pallas import tpu as pltpu

D_MODEL = 1024
N_HEADS = 8
HEAD_DIM = 64
HEAD_WIDTH = 2 * HEAD_DIM
N_SGU_GROUPS = 8
SGU_CHUNK = 128
N_GROUPS = 4
EXPERTS_PER_GROUP = 8
N_EXPERTS = N_GROUPS * EXPERTS_PER_GROUP
D_EXPERT = 256
N_SEGMENTS = 7
EPS = 1e-6
LAMBDA_INIT = 0.8 - 0.6 * math.exp(-0.3 * 0)
ALIBI_SLOPES = np.array([2.0 ** (-8.0 * (h + 1) / N_HEADS) for h in range(N_HEADS)], dtype=np.float32)
ROUTER_LANES = 128
GROUP_ID_LANE = 64
MOE_TILE = 1024
MOE_CHUNK = 128
MOE_EXPERTS_PER_STEP = 4
LOG2_E = math.log2(math.e)
IN_PROJ_SUB = 256
MIX_TILE = 512
ATTN_ROWS = 256
ATTN_KEY_TILE = 256
ATTN_L_MIN = 2.0 ** -60
ATTN_L_MAX = 2.0 ** 100

F32 = jnp.float32
BF16 = jnp.bfloat16
VMEM_LIMIT_BYTES = 56 * 1024 * 1024


def _params(semantics):
    return pltpu.CompilerParams(dimension_semantics=semantics, vmem_limit_bytes=VMEM_LIMIT_BYTES)


def _dot(a, b):
    return jnp.dot(a, b, preferred_element_type=F32)


def _sigmoid(x):
    return 1.0 / (1.0 + jnp.exp(-x))


def _gelu_tanh(x):
    return 0.5 * x * (1.0 + jnp.tanh(math.sqrt(2.0 / math.pi) * (x + 0.044715 * (x * x * x))))


def _ada_kernel(c_ref, w_ref, b_ref, lq1_ref, lk1_ref, lq2_ref, lk2_ref, mod_ref, lam_ref):
    c = c_ref[...]
    act = c * _sigmoid(c)
    mod_ref[...] = jnp.dot(act, w_ref[...], preferred_element_type=F32,
                           precision=lax.Precision.HIGHEST) + b_ref[...]
    d1 = jnp.sum(lq1_ref[...] * lk1_ref[...], axis=-1, keepdims=True)
    d2 = jnp.sum(lq2_ref[...] * lk2_ref[...], axis=-1, keepdims=True)
    lam_ref[...] = jnp.exp(d1) - jnp.exp(d2) + LAMBDA_INIT


def _ada(c, w_ada, b_ada, lq1, lk1, lq2, lk2):
    batch, d = c.shape
    n = w_ada.shape[1]
    tn = 1024
    vec = pl.BlockSpec((1, HEAD_DIM), lambda j: (0, 0))
    return pl.pallas_call(
        _ada_kernel,
        grid=(n // tn,),
        in_specs=[pl.BlockSpec((batch, d), lambda j: (0, 0)),
                  pl.BlockSpec((d, tn), lambda j: (0, j)),
                  pl.BlockSpec((1, tn), lambda j: (0, j)),
                  vec, vec, vec, vec],
        out_specs=[pl.BlockSpec((batch, tn), lambda j: (0, j)),
                   pl.BlockSpec((1, 1), lambda j: (0, 0))],
        out_shape=[jax.ShapeDtypeStruct((batch, n), F32), jax.ShapeDtypeStruct((1, 1), F32)],
        compiler_params=_params(("arbitrary",)),
        name="ada",
    )(c, w_ada, b_ada.reshape(1, n), lq1.reshape(1, -1), lk1.reshape(1, -1),
      lq2.reshape(1, -1), lk2.reshape(1, -1))


def _in_proj_kernel(x_ref, mod_ref, g_ref, w_ref, o_ref, h_scr):
    j = pl.program_id(1)

    @pl.when(j == 0)
    def _():
        x = x_ref[...]
        y = x * lax.rsqrt(jnp.mean(x * x, axis=-1, keepdims=True) + EPS) * g_ref[...]
        h_scr[...] = (y * (1.0 + mod_ref[0, 1:2, :]) + mod_ref[0, 0:1, :]).astype(BF16)

    def project(epilogue):
        for n in range(w_ref.shape[1] // IN_PROJ_SUB):
            cs = slice(n * IN_PROJ_SUB, (n + 1) * IN_PROJ_SUB)
            o_ref[0, :, cs] = epilogue(_dot(h_scr[...], w_ref[:, cs])).astype(BF16)

    @pl.when(j == 0)
    def _():
        project(lambda acc: acc * (HEAD_DIM ** -0.5 * LOG2_E))

    @pl.when((j == 1) | (j == 2))
    def _():
        project(lambda acc: acc)

    @pl.when((j == 3) | (j == 4))
    def _():
        project(_gelu_tanh)

    @pl.when(j >= 5)
    def _():
        project(_sigmoid)


def _in_proj(x2d, mod3, norm_g, w_in_bf, seq):
    t, d = x2d.shape
    tm = 1024
    tiles_per_batch = seq // tm
    return pl.pallas_call(
        _in_proj_kernel,
        grid=(t // tm, N_SEGMENTS),
        in_specs=[pl.BlockSpec((tm, d), lambda i, j: (i, 0)),
                  pl.BlockSpec((1, 6, d), lambda i, j: (i // tiles_per_batch, 0, 0)),
                  pl.BlockSpec((1, d), lambda i, j: (0, 0)),
                  pl.BlockSpec((d, d), lambda i, j: (0, j))],
        out_specs=pl.BlockSpec((1, tm, d), lambda i, j: (j, i, 0)),
        out_shape=jax.ShapeDtypeStruct((N_SEGMENTS, t, d), BF16),
        scratch_shapes=[pltpu.VMEM((tm, d), BF16)],
        compiler_params=_params(("arbitrary", "arbitrary")),
        name="in_proj",
    )(x2d, mod3, norm_g.reshape(1, d), w_in_bf)


def _attn_epilogue(o0, o1, lam, subg):
    o = o0 - lam * o1
    o = o * lax.rsqrt(jnp.mean(o * o, axis=-1, keepdims=True) + EPS)
    return (o * subg * (1.0 - LAMBDA_INIT)).astype(BF16)


def _attn_kernel(slopes_ref, lam_ref, q_ref, k_ref, v_ref, subg_ref, o_ref,
                 bias_scr, p_scr, *, rows, seq):
    h = pl.program_id(0)
    b = pl.program_id(1)
    nq = seq // rows
    contract_last = (((1,), (1,)), ((), ()))

    @pl.when(b == 0)
    def _():
        slope = -LOG2_E * slopes_ref[h]
        for qb in range(nq):
            qpos = qb * rows + lax.broadcasted_iota(jnp.int32, (rows, seq), 0)
            kpos = lax.broadcasted_iota(jnp.int32, (rows, seq), 1)
            bias_scr[qb] = slope * jnp.abs(qpos - kpos).astype(F32)

    lam = lam_ref[0, 0]
    subg = subg_ref[...]

    def masked_q(r0):
        q = q_ref[0, pl.ds(r0, rows), :]
        lane = lax.broadcasted_iota(jnp.int32, q.shape, 1)
        zero = jnp.zeros_like(q)
        return jnp.where(lane < HEAD_DIM, q, zero), jnp.where(lane >= HEAD_DIM, q, zero)

    def fast_block(qb, n_bad):
        r0 = pl.multiple_of(qb * rows, rows)
        q0, q1 = masked_q(r0)
        qq = jnp.concatenate([q0, q1], axis=0)
        lsum = jnp.zeros((2 * rows, HEAD_WIDTH), F32)
        for c in range(seq // ATTN_KEY_TILE):
            ks = slice(c * ATTN_KEY_TILE, (c + 1) * ATTN_KEY_TILE)
            s = lax.dot_general(qq, k_ref[0, ks, :], contract_last, preferred_element_type=F32)
            bias = bias_scr[qb, :, ks]
            p = jnp.exp2(s + jnp.concatenate([bias, bias], axis=0))
            for jl in range(ATTN_KEY_TILE // HEAD_WIDTH):
                lsum = lsum + p[:, jl * HEAD_WIDTH:(jl + 1) * HEAD_WIDTH]
            p_scr[:, ks] = p.astype(BF16)
        l = jnp.sum(lsum, axis=-1, keepdims=True)
        l0 = l[:rows]
        l1 = l[rows:]
        ratio = (lam * l0 / l1).astype(BF16)
        a = p_scr[:rows, :] - p_scr[rows:, :] * ratio
        o = _dot(a, v_ref[0]) / l0
        o = o * lax.rsqrt(jnp.mean(o * o, axis=-1, keepdims=True) + EPS)
        o_ref[pl.ds(r0, rows), :] = (o * subg * (1.0 - LAMBDA_INIT)).astype(BF16)
        ok = ((l0 >= ATTN_L_MIN) & (l0 <= ATTN_L_MAX)) & ((l1 >= ATTN_L_MIN) & (l1 <= ATTN_L_MAX))
        return n_bad + jnp.where(ok, 0.0, 1.0)

    n_bad = lax.fori_loop(0, nq, fast_block, jnp.zeros((rows, 1), F32), unroll=True)

    @pl.when(jnp.sum(n_bad) > 0.0)
    def _():
        def safe_block(qb, carry):
            r0 = pl.multiple_of(qb * rows, rows)
            outs = []
            for qm in masked_q(r0):
                s = lax.dot_general(qm, k_ref[0], contract_last, preferred_element_type=F32) + bias_scr[qb]
                p = jnp.exp2(s - jnp.max(s, axis=-1, keepdims=True))
                outs.append(_dot(p.astype(BF16), v_ref[0]) / jnp.sum(p, axis=-1, keepdims=True))
            o_ref[pl.ds(r0, rows), :] = _attn_epilogue(outs[0], outs[1], lam, subg)
            return carry

        lax.fori_loop(0, nq, safe_block, 0)


def _attention(proj, lam, subln_g, batch, seq):
    t = batch * seq
    rows = ATTN_ROWS
    kernel = functools.partial(_attn_kernel, rows=rows, seq=seq)
    smem = pl.BlockSpec(memory_space=pltpu.SMEM)
    seg = lambda k: pl.BlockSpec((1, seq, HEAD_WIDTH), lambda h, b, k=k: (k, b, h))
    return pl.pallas_call(
        kernel,
        grid=(N_HEADS, batch),
        in_specs=[smem, smem, seg(0), seg(1), seg(2),
                  pl.BlockSpec((1, HEAD_WIDTH), lambda h, b: (0, 0))],
        out_specs=pl.BlockSpec((seq, HEAD_WIDTH), lambda h, b: (b, h)),
        out_shape=jax.ShapeDtypeStruct((t, N_HEADS * HEAD_WIDTH), BF16),
        scratch_shapes=[pltpu.VMEM((seq // rows, rows, seq), F32),
                        pltpu.VMEM((2 * rows, seq), BF16)],
        compiler_params=_params(("arbitrary", "arbitrary")),
        name="diff_attn",
    )(jnp.asarray(ALIBI_SLOPES), lam, proj, proj, proj, subln_g.reshape(1, HEAD_WIDTH))


def _mix_kernel(attn_ref, u_ref, s_ref, ga_ref, gb_ref, x_ref, mod_ref,
                wap_ref, lng_ref, lnb_ref, ws_ref, bs_ref, wsp_ref, wout_ref,
                n2g_ref, wr_ref, br_ref,
                x1_ref, h2_ref, logit_ref, gated_scr):
    tm = x_ref.shape[0]
    y_attn = _dot(attn_ref[...], wap_ref[...])

    s = s_ref[0].astype(F32)
    mu = jnp.mean(s, axis=-1, keepdims=True)
    sc = s - mu
    var = jnp.mean(sc * sc, axis=-1, keepdims=True)
    v = ((sc * lax.rsqrt(var + EPS)) * lng_ref[...] + lnb_ref[...]).astype(BF16)
    n_chunks = tm // SGU_CHUNK
    for g in range(N_SGU_GROUPS):
        cols = slice(g * SGU_CHUNK, (g + 1) * SGU_CHUNK)
        v_g = jnp.concatenate([v[c * SGU_CHUNK:(c + 1) * SGU_CHUNK, cols] for c in range(n_chunks)], axis=1)
        mixed_g = _dot(ws_ref[g], v_g)
        for c in range(n_chunks):
            rows = slice(c * SGU_CHUNK, (c + 1) * SGU_CHUNK)
            mixed = mixed_g[:, c * SGU_CHUNK:(c + 1) * SGU_CHUNK] + bs_ref[g]
            gated_scr[rows, cols] = (u_ref[0, rows, cols].astype(F32) * mixed).astype(BF16)
    y_sgu = _dot(gated_scr[...], wsp_ref[...])

    y = ga_ref[0].astype(F32) * y_attn + gb_ref[0].astype(F32) * y_sgu
    x1 = x_ref[...] + mod_ref[0, 2:3, :] * _dot(y.astype(BF16), wout_ref[...])
    x1_ref[...] = x1

    h2 = x1 * lax.rsqrt(jnp.mean(x1 * x1, axis=-1, keepdims=True) + EPS) * n2g_ref[...]
    h2 = h2 * (1.0 + mod_ref[0, 4:5, :]) + mod_ref[0, 3:4, :]
    h2_hi = h2.astype(BF16)
    h2_ref[...] = h2_hi
    h2_lo = (h2 - h2_hi.astype(F32)).astype(BF16)
    parts = _dot(jnp.concatenate([h2_hi, h2_lo], axis=0), wr_ref[...])
    logit_ref[...] = ((parts[:tm, :ROUTER_LANES] + parts[:tm, ROUTER_LANES:])
                      + (parts[tm:, :ROUTER_LANES] + parts[tm:, ROUTER_LANES:]) + br_ref[...])


def _mix(attn, proj, x2d, mod3, wap, lng, lnb, ws, bs_full, wsp, wout, n2g, wr, br, seq):
    t, d = x2d.shape
    tm = MIX_TILE
    tiles_per_batch = seq // tm
    const2 = lambda shape: pl.BlockSpec(shape, lambda i: (0, 0))
    const3 = lambda shape: pl.BlockSpec(shape, lambda i: (0, 0, 0))
    seg = lambda k: pl.BlockSpec((1, tm, d), lambda i, k=k: (k, i, 0))
    row = pl.BlockSpec((tm, d), lambda i: (i, 0))
    return pl.pallas_call(
        _mix_kernel,
        grid=(t // tm,),
        in_specs=[row, seg(3), seg(4), seg(5), seg(6), row,
                  pl.BlockSpec((1, 6, d), lambda i: (i // tiles_per_batch, 0, 0)),
                  const2((d, d)), const2((1, d)), const2((1, d)),
                  const3((N_SGU_GROUPS, SGU_CHUNK, SGU_CHUNK)),
                  const3((N_SGU_GROUPS, SGU_CHUNK, SGU_CHUNK)),
                  const2((d, d)), const2((d, d)), const2((1, d)),
                  const2((d, 2 * ROUTER_LANES)), const2((1, ROUTER_LANES))],
        out_specs=[row, row, pl.BlockSpec((tm, ROUTER_LANES), lambda i: (i, 0))],
        out_shape=[jax.ShapeDtypeStruct((t, d), F32), jax.ShapeDtypeStruct((t, d), BF16),
                   jax.ShapeDtypeStruct((t, ROUTER_LANES), F32)],
        scratch_shapes=[pltpu.VMEM((tm, d), BF16)],
        compiler_params=_params(("arbitrary",)),
        name="mix",
    )(attn, proj, proj, proj, proj, x2d, mod3, wap, lng, lnb, ws, bs_full, wsp, wout, n2g, wr, br)


def _route_kernel(logit_ref, comb_ref, combt_ref):
    z = logit_ref[...]
    lane = lax.broadcasted_iota(jnp.int32, z.shape, 1)
    neg = jnp.float32(-jnp.inf)
    big = jnp.int32(ROUTER_LANES)

    def first_argmax(val, vmax):
        return jnp.min(jnp.where(val == vmax, lane, big), axis=-1, keepdims=True)

    is_group = (lane >= N_EXPERTS) & (lane < N_EXPERTS + N_GROUPS)
    gl = jnp.where(is_group, z, neg)
    gmax = jnp.max(gl, axis=-1, keepdims=True)
    ge = jnp.exp(gl - gmax)
    gp = ge / jnp.sum(ge, axis=-1, keepdims=True)
    gval = jnp.max(gp, axis=-1, keepdims=True)
    gidx = first_argmax(gp, gval) - N_EXPERTS

    in_group = (lane >= gidx * EXPERTS_PER_GROUP) & (lane < (gidx + 1) * EXPERTS_PER_GROUP)
    el = jnp.where(in_group, z, neg)
    emax = jnp.max(el, axis=-1, keepdims=True)
    ee = jnp.exp(el - emax)
    ep = ee / jnp.sum(ee, axis=-1, keepdims=True)
    ep = jnp.where(in_group, ep, -1.0)
    ev0 = jnp.max(ep, axis=-1, keepdims=True)
    ei0 = first_argmax(ep, ev0)
    ep_rest = jnp.where(lane == ei0, -1.0, ep)
    ev1 = jnp.max(ep_rest, axis=-1, keepdims=True)
    ei1 = first_argmax(ep_rest, ev1)
    denom = ev0 + ev1
    comb = jnp.where(lane == ei0, ev0 / denom * gval,
                     jnp.where(lane == ei1, ev1 / denom * gval,
                               jnp.where(lane == GROUP_ID_LANE, gidx.astype(F32), 0.0)))
    comb_ref[...] = comb
    combt_ref[...] = comb.T


def _route(logits):
    t = logits.shape[0]
    tm = 1024
    spec = pl.BlockSpec((tm, ROUTER_LANES), lambda i: (i, 0))
    return pl.pallas_call(
        _route_kernel, grid=(t // tm,), in_specs=[spec],
        out_specs=[spec, pl.BlockSpec((ROUTER_LANES, tm), lambda i: (0, i))],
        out_shape=[jax.ShapeDtypeStruct((t, ROUTER_LANES), F32), jax.ShapeDtypeStruct((ROUTER_LANES, t), F32)],
        compiler_params=_params(("arbitrary",)), name="route",
    )(logits)


def _moe_kernel(h2_ref, comb_ref, combt_ref, wgu_ref, wd_ref, x1_ref, mod_ref, fg_ref, out_ref,
                onehot_scr, xs_scr, wl_scr, outc_scr, meta_ref):
    j = pl.program_id(1)
    tt = h2_ref.shape[0]
    rc = onehot_scr.shape[0]
    steps_per_group = EXPERTS_PER_GROUP // MOE_EXPERTS_PER_STEP
    g = j // steps_per_group

    @pl.when(j == 0)
    def _():
        gid = combt_ref[GROUP_ID_LANE:GROUP_ID_LANE + 1, :]
        sub = lax.broadcasted_iota(jnp.int32, (8, tt), 0).astype(F32)
        member = jnp.where(gid == sub, 1.0, 0.0)
        before = jnp.where(lax.broadcasted_iota(jnp.int32, (tt, tt), 0)
                           < lax.broadcasted_iota(jnp.int32, (tt, tt), 1), 1.0, 0.0).astype(BF16)
        rank = _dot(member.astype(BF16), before)
        pos = jnp.zeros((1, tt), F32)
        off = jnp.int32(0)
        for gg in range(N_GROUPS):
            count = jnp.sum(member[gg:gg + 1, :]).astype(jnp.int32)
            n_chunks = (count + (MOE_CHUNK - 1)) // MOE_CHUNK
            meta_ref[gg] = off
            meta_ref[N_GROUPS + gg] = n_chunks
            pos = pos + member[gg:gg + 1, :] * (off.astype(F32) + rank[gg:gg + 1, :])
            off = off + n_chunks * MOE_CHUNK
        row = lax.broadcasted_iota(jnp.int32, (rc, tt), 0).astype(F32)
        onehot = jnp.where(row == pos, 1.0, 0.0).astype(BF16)
        onehot_scr[...] = onehot
        xs_scr[...] = _dot(onehot, h2_ref[...]).astype(BF16)
        comb = comb_ref[...]
        c_hi = comb.astype(BF16)
        c_lo = (comb - c_hi.astype(F32)).astype(BF16)
        wl2 = _dot(onehot, jnp.concatenate([c_hi, c_lo], axis=1))
        wl_scr[...] = wl2[:, :ROUTER_LANES] + wl2[:, ROUTER_LANES:]
        outc_scr[...] = jnp.zeros_like(outc_scr)

    off_g = meta_ref[g]
    n_chunks_g = meta_ref[N_GROUPS + g]

    def chunk(k, carry):
        r0 = pl.multiple_of(off_g + k * MOE_CHUNK, MOE_CHUNK)
        xs = xs_scr[pl.ds(r0, MOE_CHUNK), :]
        wl = wl_scr[pl.ds(r0, MOE_CHUNK), :]
        lane = lax.broadcasted_iota(jnp.int32, wl.shape, 1)
        acts = []
        for e in range(MOE_EXPERTS_PER_STEP):
            gu = _dot(xs, wgu_ref[e])
            gate = gu[:, :D_EXPERT]
            up = gu[:, D_EXPERT:]
            w = jnp.sum(jnp.where(lane == j * MOE_EXPERTS_PER_STEP + e, wl, 0.0), axis=-1, keepdims=True)
            acts.append((gate * _sigmoid(gate) * up * w).astype(BF16))
        wd_all = wd_ref[...].reshape(MOE_EXPERTS_PER_STEP * D_EXPERT, D_MODEL)
        outc_scr[pl.ds(r0, MOE_CHUNK), :] += _dot(jnp.concatenate(acts, axis=1), wd_all)
        return carry

    lax.fori_loop(0, n_chunks_g, chunk, 0)

    @pl.when(j == pl.num_programs(1) - 1)
    def _():
        y = lax.dot_general(onehot_scr[...], outc_scr[...].astype(BF16), (((0,), (0,)), ((), ())),
                            preferred_element_type=F32)
        x2 = x1_ref[...] + mod_ref[0, 5:6, :] * y
        out_ref[...] = x2 * lax.rsqrt(jnp.mean(x2 * x2, axis=-1, keepdims=True) + EPS) * fg_ref[...]


def _moe(h2, comb, combt, wgu, wd, x1, mod3, final_g, seq):
    t, d = x1.shape
    tm = MOE_TILE
    tiles_per_batch = seq // tm
    n_steps = N_EXPERTS // MOE_EXPERTS_PER_STEP
    compact_rows = tm + N_GROUPS * MOE_CHUNK
    row = pl.BlockSpec((tm, d), lambda i, j: (i, 0))
    return pl.pallas_call(
        _moe_kernel,
        grid=(t // tm, n_steps),
        in_specs=[row,
                  pl.BlockSpec((tm, ROUTER_LANES), lambda i, j: (i, 0)),
                  pl.BlockSpec((ROUTER_LANES, tm), lambda i, j: (0, i)),
                  pl.BlockSpec((MOE_EXPERTS_PER_STEP, d, 2 * D_EXPERT), lambda i, j: (j, 0, 0)),
                  pl.BlockSpec((MOE_EXPERTS_PER_STEP, D_EXPERT, d), lambda i, j: (j, 0, 0)),
                  pl.BlockSpec((tm, d), lambda i, j: (i, 0), pipeline_mode=pl.Buffered(1)),
                  pl.BlockSpec((1, 6, d), lambda i, j: (i // tiles_per_batch, 0, 0)),
                  pl.BlockSpec((1, d), lambda i, j: (0, 0))],
        out_specs=row,
        out_shape=jax.ShapeDtypeStruct((t, d), F32),
        scratch_shapes=[pltpu.VMEM((compact_rows, tm), BF16),
                        pltpu.VMEM((compact_rows, d), BF16),
                        pltpu.VMEM((compact_rows, ROUTER_LANES), F32),
                        pltpu.VMEM((compact_rows, d), F32),
                        pltpu.SMEM((2 * N_GROUPS,), jnp.int32)],
        compiler_params=_params(("arbitrary", "arbitrary")),
        name="moe",
    )(h2, comb, combt, wgu, wd, x1, mod3, final_g.reshape(1, d))


def _router_weights(w_rg, b_rg, w_re, b_re):
    d = w_rg.shape[0]
    pad = ROUTER_LANES - N_EXPERTS - N_GROUPS
    w = jnp.concatenate([w_re, w_rg, jnp.zeros((d, pad), F32)], axis=1)
    b = jnp.concatenate([b_re, b_rg, jnp.zeros((pad,), F32)]).reshape(1, ROUTER_LANES)
    w_hi = w.astype(BF16)
    w_lo = (w - w_hi.astype(F32)).astype(BF16)
    return jnp.concatenate([w_hi, w_lo], axis=1), b


def kernel(x, c, w_ada, b_ada, norm1_g, w_in, lambda_q1, lambda_k1, lambda_q2, lambda_k2, subln_g, w_attn_proj, sgu_ln_g, sgu_ln_b, sgu_w_s, sgu_b_s, w_sgu_proj, w_out, norm2_g, w_router_group, b_router_group, w_router_expert, b_router_expert, w_expert_gate_up, w_expert_down, final_g):
    batch, seq, d = x.shape
    assert w_ada.shape[0] == 1, "single-layer trunk"
    x2d = x.reshape(batch * seq, d)

    mod, lam = _ada(c, w_ada[0], b_ada[0], lambda_q1[0], lambda_k1[0], lambda_q2[0], lambda_k2[0])
    mod3 = mod.reshape(batch, 6, d)

    proj = _in_proj(x2d, mod3, norm1_g[0], w_in[0].astype(BF16), seq)
    attn = _attention(proj, lam, subln_g[0], batch, seq)

    bs_full = jnp.broadcast_to(sgu_b_s[0][:, :, None], (N_SGU_GROUPS, SGU_CHUNK, SGU_CHUNK))
    wr, br = _router_weights(w_router_group[0], b_router_group[0], w_router_expert[0], b_router_expert[0])
    x1, h2, logits = _mix(attn, proj, x2d, mod3, w_attn_proj[0].astype(BF16),
                          sgu_ln_g[0].reshape(1, d), sgu_ln_b[0].reshape(1, d),
                          sgu_w_s[0].astype(BF16), bs_full, w_sgu_proj[0].astype(BF16),
                          w_out[0].astype(BF16), norm2_g[0].reshape(1, d), wr, br, seq)
    comb, combt = _route(logits)
    out = _moe(h2, comb, combt, w_expert_gate_up[0].astype(BF16), w_expert_down[0].astype(BF16),
               x1, mod3, final_g, seq)
    return out.reshape(batch, seq, d)
```

```python
import functools
import math

import jax
import jax.numpy as jnp
import numpy as np
from jax import lax
from jax.experimental import pallas as pl
from jax.experimental.pallas import tpu as pltpu

D_MODEL = 1024
N_HEADS = 8
HEAD_DIM = 64
HEAD_WIDTH = 2 * HEAD_DIM
N_SGU_GROUPS = 8
SGU_CHUNK = 128
N_GROUPS = 4
EXPERTS_PER_GROUP = 8
N_EXPERTS = N_GROUPS * EXPERTS_PER_GROUP
D_EXPERT = 256
N_SEGMENTS = 7
EPS = 1e-6
LAMBDA_INIT = 0.8 - 0.6 * math.exp(-0.3 * 0)
ALIBI_SLOPES = np.array([2.0 ** (-8.0 * (h + 1) / N_HEADS) for h in range(N_HEADS)], dtype=np.float32)
ROUTER_LANES = 128
GROUP_ID_LANE = 64
MOE_TILE = 1024
MOE_CHUNK = 128
MOE_EXPERTS_PER_STEP = 4
LOG2_E = math.log2(math.e)
IN_PROJ_STEPS = 4
IN_PROJ_SUB = 256
MIX_TILE = 512
ATTN_ROWS = 256
ATTN_KEY_TILE = 256
ATTN_L_MIN = 2.0 ** -60
ATTN_L_MAX = 2.0 ** 100

F32 = jnp.float32
BF16 = jnp.bfloat16
VMEM_LIMIT_BYTES = 56 * 1024 * 1024


def _params(semantics):
    return pltpu.CompilerParams(dimension_semantics=semantics, vmem_limit_bytes=VMEM_LIMIT_BYTES)


def _dot(a, b):
    return jnp.dot(a, b, preferred_element_type=F32)


def _sigmoid(x):
    return 1.0 / (1.0 + jnp.exp(-x))


def _gelu_tanh(x):
    return 0.5 * x * (1.0 + jnp.tanh(math.sqrt(2.0 / math.pi) * (x + 0.044715 * (x * x * x))))


def _ada_kernel(c_ref, w_ref, b_ref, lq1_ref, lk1_ref, lq2_ref, lk2_ref, mod_ref, lam_ref):
    c = c_ref[...]
    act = c * _sigmoid(c)
    mod_ref[...] = jnp.dot(act, w_ref[...], preferred_element_type=F32,
                           precision=lax.Precision.HIGHEST) + b_ref[...]
    d1 = jnp.sum(lq1_ref[...] * lk1_ref[...], axis=-1, keepdims=True)
    d2 = jnp.sum(lq2_ref[...] * lk2_ref[...], axis=-1, keepdims=True)
    lam_ref[...] = jnp.exp(d1) - jnp.exp(d2) + LAMBDA_INIT


def _ada(c, w_ada, b_ada, lq1, lk1, lq2, lk2):
    batch, d = c.shape
    n = w_ada.shape[1]
    tn = 1024
    vec = pl.BlockSpec((1, HEAD_DIM), lambda j: (0, 0))
    return pl.pallas_call(
        _ada_kernel,
        grid=(n // tn,),
        in_specs=[pl.BlockSpec((batch, d), lambda j: (0, 0)),
                  pl.BlockSpec((d, tn), lambda j: (0, j)),
                  pl.BlockSpec((1, tn), lambda j: (0, j)),
                  vec, vec, vec, vec],
        out_specs=[pl.BlockSpec((batch, tn), lambda j: (0, j)),
                   pl.BlockSpec((1, 1), lambda j: (0, 0))],
        out_shape=[jax.ShapeDtypeStruct((batch, n), F32), jax.ShapeDtypeStruct((1, 1), F32)],
        compiler_params=_params(("arbitrary",)),
        name="ada",
    )(c, w_ada, b_ada.reshape(1, n), lq1.reshape(1, -1), lk1.reshape(1, -1),
      lq2.reshape(1, -1), lk2.reshape(1, -1))


def _in_proj_kernel(x_ref, mod_ref, g_ref, wa_ref, wb_ref, oa_ref, ob_ref, h_scr):
    j = pl.program_id(1)

    @pl.when(j == 0)
    def _():
        x = x_ref[...]
        y = x * lax.rsqrt(jnp.mean(x * x, axis=-1, keepdims=True) + EPS) * g_ref[...]
        h_scr[...] = (y * (1.0 + mod_ref[0, 1:2, :]) + mod_ref[0, 0:1, :]).astype(BF16)

    def project(epilogue_a, epilogue_b):
        for n in range(wa_ref.shape[1] // IN_PROJ_SUB):
            cs = slice(n * IN_PROJ_SUB, (n + 1) * IN_PROJ_SUB)
            oa_ref[0, :, cs] = epilogue_a(_dot(h_scr[...], wa_ref[:, cs])).astype(BF16)
            if epilogue_b is not None:
                ob_ref[0, :, cs] = epilogue_b(_dot(h_scr[...], wb_ref[:, cs])).astype(BF16)

    @pl.when(j == 0)
    def _():
        project(lambda acc: acc * (HEAD_DIM ** -0.5 * LOG2_E), _gelu_tanh)

    @pl.when(j == 1)
    def _():
        project(lambda acc: acc, _gelu_tanh)

    @pl.when(j == 2)
    def _():
        project(lambda acc: acc, _sigmoid)

    @pl.when(j == 3)
    def _():
        project(_sigmoid, None)


def _in_proj(x2d, mod3, norm_g, w_in_bf, seq):
    t, d = x2d.shape
    tm = 1024
    tiles_per_batch = seq // tm
    n_b = N_SEGMENTS - IN_PROJ_STEPS
    seg_a = lambda j: jnp.where(j < n_b, j, N_SEGMENTS - 1)
    seg_b = lambda j: n_b + jnp.minimum(j, n_b - 1)
    return pl.pallas_call(
        _in_proj_kernel,
        grid=(t // tm, IN_PROJ_STEPS),
        in_specs=[pl.BlockSpec((tm, d), lambda i, j: (i, 0)),
                  pl.BlockSpec((1, 6, d), lambda i, j: (i // tiles_per_batch, 0, 0)),
                  pl.BlockSpec((1, d), lambda i, j: (0, 0)),
                  pl.BlockSpec((d, d), lambda i, j: (0, seg_a(j))),
                  pl.BlockSpec((d, d), lambda i, j: (0, seg_b(j)))],
        out_specs=[pl.BlockSpec((1, tm, d), lambda i, j: (j, i, 0)),
                   pl.BlockSpec((1, tm, d), lambda i, j: (jnp.minimum(j, n_b - 1), i, 0))],
        out_shape=[jax.ShapeDtypeStruct((IN_PROJ_STEPS, t, d), BF16), jax.ShapeDtypeStruct((n_b, t, d), BF16)],
        scratch_shapes=[pltpu.VMEM((tm, d), BF16)],
        compiler_params=_params(("arbitrary", "arbitrary")),
        name="in_proj",
    )(x2d, mod3, norm_g.reshape(1, d), w_in_bf, w_in_bf)


def _attn_epilogue(o0, o1, lam, subg):
    o = o0 - lam * o1
    o = o * lax.rsqrt(jnp.mean(o * o, axis=-1, keepdims=True) + EPS)
    return (o * subg * (1.0 - LAMBDA_INIT)).astype(BF16)


def _attn_kernel(slopes_ref, lam_ref, q_ref, k_ref, v_ref, subg_ref, o_ref,
                 bias_scr, p_scr, vext_scr, *, rows, seq):
    h = pl.program_id(0)
    b = pl.program_id(1)
    nq = seq // rows
    contract_last = (((1,), (1,)), ((), ()))

    @pl.when(b == 0)
    def _():
        slope = -LOG2_E * slopes_ref[h]
        for qb in range(nq):
            qpos = qb * rows + lax.broadcasted_iota(jnp.int32, (rows, seq), 0)
            kpos = lax.broadcasted_iota(jnp.int32, (rows, seq), 1)
            bias_scr[qb] = slope * jnp.abs(qpos - kpos).astype(F32)
        lane = lax.broadcasted_iota(jnp.int32, (seq, HEAD_WIDTH), 1)
        vext_scr[:, HEAD_WIDTH:] = jnp.where(lane == 0, 1.0, 0.0).astype(BF16)

    vext_scr[:, :HEAD_WIDTH] = v_ref[0]
    lam = lam_ref[0, 0]
    subg = subg_ref[...]

    def masked_q(r0):
        q = q_ref[0, pl.ds(r0, rows), :]
        lane = lax.broadcasted_iota(jnp.int32, q.shape, 1)
        zero = jnp.zeros_like(q)
        return jnp.where(lane < HEAD_DIM, q, zero), jnp.where(lane >= HEAD_DIM, q, zero)

    def fast_block(qb, n_bad):
        r0 = pl.multiple_of(qb * rows, rows)
        q0, q1 = masked_q(r0)
        qq = jnp.concatenate([q0, q1], axis=0)
        for c in range(seq // ATTN_KEY_TILE):
            ks = slice(c * ATTN_KEY_TILE, (c + 1) * ATTN_KEY_TILE)
            s = lax.dot_general(qq, k_ref[0, ks, :], contract_last, preferred_element_type=F32)
            bias = bias_scr[qb, :, ks]
            p_scr[:rows, ks] = jnp.exp2(s[:rows] + bias).astype(BF16)
            p_scr[rows:, ks] = jnp.exp2(s[rows:] + bias).astype(BF16)
        oo = _dot(p_scr[...], vext_scr[...])
        l0 = oo[:rows, HEAD_WIDTH:HEAD_WIDTH + 1]
        l1 = oo[rows:, HEAD_WIDTH:HEAD_WIDTH + 1]
        o_ref[pl.ds(r0, rows), :] = _attn_epilogue(oo[:rows, :HEAD_WIDTH] / l0, oo[rows:, :HEAD_WIDTH] / l1,
                                                   lam, subg)
        ok = ((l0 >= ATTN_L_MIN) & (l0 <= ATTN_L_MAX)) & ((l1 >= ATTN_L_MIN) & (l1 <= ATTN_L_MAX))
        return n_bad + jnp.where(ok, 0.0, 1.0)

    n_bad = lax.fori_loop(0, nq, fast_block, jnp.zeros((rows, 1), F32), unroll=True)

    @pl.when(jnp.sum(n_bad) > 0.0)
    def _():
        def safe_block(qb, carry):
            r0 = pl.multiple_of(qb * rows, rows)
            outs = []
            for qm in masked_q(r0):
                s = lax.dot_general(qm, k_ref[0], contract_last, preferred_element_type=F32) + bias_scr[qb]
                p = jnp.exp2(s - jnp.max(s, axis=-1, keepdims=True))
                outs.append(_dot(p.astype(BF16), v_ref[0]) / jnp.sum(p, axis=-1, keepdims=True))
            o_ref[pl.ds(r0, rows), :] = _attn_epilogue(outs[0], outs[1], lam, subg)
            return carry

        lax.fori_loop(0, nq, safe_block, 0)


def _attention(qkv, lam, subln_g, batch, seq):
    t = batch * seq
    rows = ATTN_ROWS
    kernel = functools.partial(_attn_kernel, rows=rows, seq=seq)
    smem = pl.BlockSpec(memory_space=pltpu.SMEM)
    seg = lambda k: pl.BlockSpec((1, seq, HEAD_WIDTH), lambda h, b, k=k: (k, b, h))
    return pl.pallas_call(
        kernel,
        grid=(N_HEADS, batch),
        in_specs=[smem, smem, seg(0), seg(1), seg(2),
                  pl.BlockSpec((1, HEAD_WIDTH), lambda h, b: (0, 0))],
        out_specs=pl.BlockSpec((seq, HEAD_WIDTH), lambda h, b: (b, h)),
        out_shape=jax.ShapeDtypeStruct((t, N_HEADS * HEAD_WIDTH), BF16),
        scratch_shapes=[pltpu.VMEM((seq // rows, rows, seq), F32),
                        pltpu.VMEM((2 * rows, seq), BF16),
                        pltpu.VMEM((seq, 2 * HEAD_WIDTH), BF16)],
        compiler_params=_params(("arbitrary", "arbitrary")),
        name="diff_attn",
    )(jnp.asarray(ALIBI_SLOPES), lam, qkv, qkv, qkv, subln_g.reshape(1, HEAD_WIDTH))


def _mix_kernel(attn_ref, u_ref, s_ref, ga_ref, gb_ref, x_ref, mod_ref,
                wap_ref, lng_ref, lnb_ref, ws_ref, bs_ref, wsp_ref, wout_ref,
                n2g_ref, wr_ref, br_ref,
                x1_ref, h2_ref, logit_ref, gated_scr):
    tm = x_ref.shape[0]
    y_attn = _dot(attn_ref[...], wap_ref[...])

    s = s_ref[0].astype(F32)
    mu = jnp.mean(s, axis=-1, keepdims=True)
    sc = s - mu
    var = jnp.mean(sc * sc, axis=-1, keepdims=True)
    v = ((sc * lax.rsqrt(var + EPS)) * lng_ref[...] + lnb_ref[...]).astype(BF16)
    n_chunks = tm // SGU_CHUNK
    for g in range(N_SGU_GROUPS):
        cols = slice(g * SGU_CHUNK, (g + 1) * SGU_CHUNK)
        v_g = jnp.concatenate([v[c * SGU_CHUNK:(c + 1) * SGU_CHUNK, cols] for c in range(n_chunks)], axis=1)
        mixed_g = _dot(ws_ref[g], v_g)
        for c in range(n_chunks):
            rows = slice(c * SGU_CHUNK, (c + 1) * SGU_CHUNK)
            mixed = mixed_g[:, c * SGU_CHUNK:(c + 1) * SGU_CHUNK] + bs_ref[g]
            gated_scr[rows, cols] = (u_ref[0, rows, cols].astype(F32) * mixed).astype(BF16)
    y_sgu = _dot(gated_scr[...], wsp_ref[...])

    y = ga_ref[0].astype(F32) * y_attn + gb_ref[0].astype(F32) * y_sgu
    x1 = x_ref[...] + mod_ref[0, 2:3, :] * _dot(y.astype(BF16), wout_ref[...])
    x1_ref[...] = x1

    h2 = x1 * lax.rsqrt(jnp.mean(x1 * x1, axis=-1, keepdims=True) + EPS) * n2g_ref[...]
    h2 = h2 * (1.0 + mod_ref[0, 4:5, :]) + mod_ref[0, 3:4, :]
    h2_hi = h2.astype(BF16)
    h2_ref[...] = h2_hi
    h2_lo = (h2 - h2_hi.astype(F32)).astype(BF16)
    parts = _dot(jnp.concatenate([h2_hi, h2_lo], axis=0), wr_ref[...])
    logit_ref[...] = ((parts[:tm, :ROUTER_LANES] + parts[:tm, ROUTER_LANES:])
                      + (parts[tm:, :ROUTER_LANES] + parts[tm:, ROUTER_LANES:]) + br_ref[...])


def _mix(attn, proj_a, proj_b, x2d, mod3, wap, lng, lnb, ws, bs_full, wsp, wout, n2g, wr, br, seq):
    t, d = x2d.shape
    tm = MIX_TILE
    tiles_per_batch = seq // tm
    const2 = lambda shape: pl.BlockSpec(shape, lambda i: (0, 0))
    const3 = lambda shape: pl.BlockSpec(shape, lambda i: (0, 0, 0))
    seg = lambda k: pl.BlockSpec((1, tm, d), lambda i, k=k: (k, i, 0))
    row = pl.BlockSpec((tm, d), lambda i: (i, 0))
    return pl.pallas_call(
        _mix_kernel,
        grid=(t // tm,),
        in_specs=[row, seg(0), seg(1), seg(2), seg(IN_PROJ_STEPS - 1), row,
                  pl.BlockSpec((1, 6, d), lambda i: (i // tiles_per_batch, 0, 0)),
                  const2((d, d)), const2((1, d)), const2((1, d)),
                  const3((N_SGU_GROUPS, SGU_CHUNK, SGU_CHUNK)),
                  const3((N_SGU_GROUPS, SGU_CHUNK, SGU_CHUNK)),
                  const2((d, d)), const2((d, d)), const2((1, d)),
                  const2((d, 2 * ROUTER_LANES)), const2((1, ROUTER_LANES))],
        out_specs=[row, row, pl.BlockSpec((tm, ROUTER_LANES), lambda i: (i, 0))],
        out_shape=[jax.ShapeDtypeStruct((t, d), F32), jax.ShapeDtypeStruct((t, d), BF16),
                   jax.ShapeDtypeStruct((t, ROUTER_LANES), F32)],
        scratch_shapes=[pltpu.VMEM((tm, d), BF16)],
        compiler_params=_params(("arbitrary",)),
        name="mix",
    )(attn, proj_b, proj_b, proj_b, proj_a, x2d, mod3, wap, lng, lnb, ws, bs_full, wsp, wout, n2g, wr, br)


def _route_kernel(logit_ref, comb_ref, combt_ref):
    z = logit_ref[...]
    lane = lax.broadcasted_iota(jnp.int32, z.shape, 1)
    neg = jnp.float32(-jnp.inf)
    big = jnp.int32(ROUTER_LANES)

    def first_argmax(val, vmax):
        return jnp.min(jnp.where(val == vmax, lane, big), axis=-1, keepdims=True)

    is_group = (lane >= N_EXPERTS) & (lane < N_EXPERTS + N_GROUPS)
    gl = jnp.where(is_group, z, neg)
    gmax = jnp.max(gl, axis=-1, keepdims=True)
    ge = jnp.exp(gl - gmax)
    gp = ge / jnp.sum(ge, axis=-1, keepdims=True)
    gval = jnp.max(gp, axis=-1, keepdims=True)
    gidx = first_argmax(gp, gval) - N_EXPERTS

    in_group = (lane >= gidx * EXPERTS_PER_GROUP) & (lane < (gidx + 1) * EXPERTS_PER_GROUP)
    el = jnp.where(in_group, z, neg)
    emax = jnp.max(el, axis=-1, keepdims=True)
    ee = jnp.exp(el - emax)
    ep = ee / jnp.sum(ee, axis=-1, keepdims=True)
    ep = jnp.where(in_group, ep, -1.0)
    ev0 = jnp.max(ep, axis=-1, keepdims=True)
    ei0 = first_argmax(ep, ev0)
    ep_rest = jnp.where(lane == ei0, -1.0, ep)
    ev1 = jnp.max(ep_rest, axis=-1, keepdims=True)
    ei1 = first_argmax(ep_rest, ev1)
    denom = ev0 + ev1
    comb = jnp.where(lane == ei0, ev0 / denom * gval,
                     jnp.where(lane == ei1, ev1 / denom * gval,
                               jnp.where(lane == GROUP_ID_LANE, gidx.astype(F32), 0.0)))
    comb_ref[...] = comb
    combt_ref[...] = comb.T


def _route(logits):
    t = logits.shape[0]
    tm = 1024
    spec = pl.BlockSpec((tm, ROUTER_LANES), lambda i: (i, 0))
    return pl.pallas_call(
        _route_kernel, grid=(t // tm,), in_specs=[spec],
        out_specs=[spec, pl.BlockSpec((ROUTER_LANES, tm), lambda i: (0, i))],
        out_shape=[jax.ShapeDtypeStruct((t, ROUTER_LANES), F32), jax.ShapeDtypeStruct((ROUTER_LANES, t), F32)],
        compiler_params=_params(("arbitrary",)), name="route",
    )(logits)


def _moe_kernel(h2_ref, comb_ref, combt_ref, wgu_ref, wd_ref, x1_ref, mod_ref, fg_ref, out_ref,
                onehot_scr, xs_scr, wl_scr, outc_scr, meta_ref):
    j = pl.program_id(1)
    tt = h2_ref.shape[0]
    rc = onehot_scr.shape[0]
    steps_per_group = EXPERTS_PER_GROUP // MOE_EXPERTS_PER_STEP
    g = j // steps_per_group

    @pl.when(j == 0)
    def _():
        gid = combt_ref[GROUP_ID_LANE:GROUP_ID_LANE + 1, :]
        sub = lax.broadcasted_iota(jnp.int32, (8, tt), 0).astype(F32)
        member = jnp.where(gid == sub, 1.0, 0.0)
        before = jnp.where(lax.broadcasted_iota(jnp.int32, (tt, tt), 0)
                           < lax.broadcasted_iota(jnp.int32, (tt, tt), 1), 1.0, 0.0).astype(BF16)
        rank = _dot(member.astype(BF16), before)
        pos = jnp.zeros((1, tt), F32)
        off = jnp.int32(0)
        for gg in range(N_GROUPS):
            count = jnp.sum(member[gg:gg + 1, :]).astype(jnp.int32)
            n_chunks = (count + (MOE_CHUNK - 1)) // MOE_CHUNK
            meta_ref[gg] = off
            meta_ref[N_GROUPS + gg] = n_chunks
            pos = pos + member[gg:gg + 1, :] * (off.astype(F32) + rank[gg:gg + 1, :])
            off = off + n_chunks * MOE_CHUNK
        row = lax.broadcasted_iota(jnp.int32, (rc, tt), 0).astype(F32)
        onehot = jnp.where(row == pos, 1.0, 0.0).astype(BF16)
        onehot_scr[...] = onehot
        xs_scr[...] = _dot(onehot, h2_ref[...]).astype(BF16)
        comb = comb_ref[...]
        c_hi = comb.astype(BF16)
        c_lo = (comb - c_hi.astype(F32)).astype(BF16)
        wl2 = _dot(onehot, jnp.concatenate([c_hi, c_lo], axis=1))
        wl_scr[...] = wl2[:, :ROUTER_LANES] + wl2[:, ROUTER_LANES:]
        outc_scr[...] = jnp.zeros_like(outc_scr)

    off_g = meta_ref[g]
    n_chunks_g = meta_ref[N_GROUPS + g]

    def experts(r0, rows):
        xs = xs_scr[pl.ds(r0, rows), :]
        wl = wl_scr[pl.ds(r0, rows), :]
        lane = lax.broadcasted_iota(jnp.int32, wl.shape, 1)
        acts = []
        for e in range(MOE_EXPERTS_PER_STEP):
            gu = _dot(xs, wgu_ref[e])
            gate = gu[:, :D_EXPERT]
            up = gu[:, D_EXPERT:]
            w = jnp.sum(jnp.where(lane == j * MOE_EXPERTS_PER_STEP + e, wl, 0.0), axis=-1, keepdims=True)
            acts.append((gate * _sigmoid(gate) * up * w).astype(BF16))
        wd_all = wd_ref[...].reshape(MOE_EXPERTS_PER_STEP * D_EXPERT, D_MODEL)
        outc_scr[pl.ds(r0, rows), :] += _dot(jnp.concatenate(acts, axis=1), wd_all)

    def chunk_pair(k, carry):
        experts(pl.multiple_of(off_g + k * (2 * MOE_CHUNK), MOE_CHUNK), 2 * MOE_CHUNK)
        return carry

    lax.fori_loop(0, n_chunks_g // 2, chunk_pair, 0)

    @pl.when(n_chunks_g % 2 == 1)
    def _():
        experts(pl.multiple_of(off_g + (n_chunks_g - 1) * MOE_CHUNK, MOE_CHUNK), MOE_CHUNK)

    @pl.when(j == pl.num_programs(1) - 1)
    def _():
        y = lax.dot_general(onehot_scr[...], outc_scr[...].astype(BF16), (((0,), (0,)), ((), ())),
                            preferred_element_type=F32)
        x2 = x1_ref[...] + mod_ref[0, 5:6, :] * y
        out_ref[...] = x2 * lax.rsqrt(jnp.mean(x2 * x2, axis=-1, keepdims=True) + EPS) * fg_ref[...]


def _moe(h2, comb, combt, wgu, wd, x1, mod3, final_g, seq):
    t, d = x1.shape
    tm = MOE_TILE
    tiles_per_batch = seq // tm
    n_steps = N_EXPERTS // MOE_EXPERTS_PER_STEP
    compact_rows = tm + N_GROUPS * MOE_CHUNK
    row = pl.BlockSpec((tm, d), lambda i, j: (i, 0))
    return pl.pallas_call(
        _moe_kernel,
        grid=(t // tm, n_steps),
        in_specs=[row,
                  pl.BlockSpec((tm, ROUTER_LANES), lambda i, j: (i, 0)),
                  pl.BlockSpec((ROUTER_LANES, tm), lambda i, j: (0, i)),
                  pl.BlockSpec((MOE_EXPERTS_PER_STEP, d, 2 * D_EXPERT), lambda i, j: (j, 0, 0)),
                  pl.BlockSpec((MOE_EXPERTS_PER_STEP, D_EXPERT, d), lambda i, j: (j, 0, 0)),
                  pl.BlockSpec((tm, d), lambda i, j: (i, 0), pipeline_mode=pl.Buffered(1)),
                  pl.BlockSpec((1, 6, d), lambda i, j: (i // tiles_per_batch, 0, 0)),
                  pl.BlockSpec((1, d), lambda i, j: (0, 0))],
        out_specs=row,
        out_shape=jax.ShapeDtypeStruct((t, d), F32),
        scratch_shapes=[pltpu.VMEM((compact_rows, tm), BF16),
                        pltpu.VMEM((compact_rows, d), BF16),
                        pltpu.VMEM((compact_rows, ROUTER_LANES), F32),
                        pltpu.VMEM((compact_rows, d), F32),
                        pltpu.SMEM((2 * N_GROUPS,), jnp.int32)],
        compiler_params=_params(("arbitrary", "arbitrary")),
        name="moe",
    )(h2, comb, combt, wgu, wd, x1, mod3, final_g.reshape(1, d))


def _router_weights(w_rg, b_rg, w_re, b_re):
    d = w_rg.shape[0]
    pad = ROUTER_LANES - N_EXPERTS - N_GROUPS
    w = jnp.concatenate([w_re, w_rg, jnp.zeros((d, pad), F32)], axis=1)
    b = jnp.concatenate([b_re, b_rg, jnp.zeros((pad,), F32)]).reshape(1, ROUTER_LANES)
    w_hi = w.astype(BF16)
    w_lo = (w - w_hi.astype(F32)).astype(BF16)
    return jnp.concatenate([w_hi, w_lo], axis=1), b


def kernel(x, c, w_ada, b_ada, norm1_g, w_in, lambda_q1, lambda_k1, lambda_q2, lambda_k2, subln_g, w_attn_proj, sgu_ln_g, sgu_ln_b, sgu_w_s, sgu_b_s, w_sgu_proj, w_out, norm2_g, w_router_group, b_router_group, w_router_expert, b_router_expert, w_expert_gate_up, w_expert_down, final_g):
    batch, seq, d = x.shape
    assert w_ada.shape[0] == 1, "single-layer trunk"
    x2d = x.reshape(batch * seq, d)

    mod, lam = _ada(c, w_ada[0], b_ada[0], lambda_q1[0], lambda_k1[0], lambda_q2[0], lambda_k2[0])
    mod3 = mod.reshape(batch, 6, d)

    proj_a, proj_b = _in_proj(x2d, mod3, norm1_g[0], w_in[0].astype(BF16), seq)
    attn = _attention(proj_a, lam, subln_g[0], batch, seq)

    bs_full = jnp.broadcast_to(sgu_b_s[0][:, :, None], (N_SGU_GROUPS, SGU_CHUNK, SGU_CHUNK))
    wr, br = _router_weights(w_router_group[0], b_router_group[0], w_router_expert[0], b_router_expert[0])
    x1, h2, logits = _mix(attn, proj_a, proj_b, x2d, mod3, w_attn_proj[0].astype(BF16),
                          sgu_ln_g[0].reshape(1, d), sgu_ln_b[0].reshape(1, d),
                          sgu_w_s[0].astype(BF16), bs_full, w_sgu_proj[0].astype(BF16),
                          w_out[0].astype(BF16), norm2_g[0].reshape(1, d), wr, br, seq)
    comb, combt = _route(logits)
    out = _moe(h2, comb, combt, w_expert_gate_up[0].astype(BF16), w_expert_down[0].astype(BF16),
               x1, mod3, final_g, seq)
    return out.reshape(batch, seq, d)
```

```python
import functools
import math

import jax
import jax.numpy as jnp
import numpy as np
from jax import lax
from jax.experimental import pallas as pl
from jax.experimental.pallas import tpu as pltpu

D_MODEL = 1024
N_HEADS = 8
HEAD_DIM = 64
HEAD_WIDTH = 2 * HEAD_DIM
N_SGU_GROUPS = 8
SGU_CHUNK = 128
N_GROUPS = 4
EXPERTS_PER_GROUP = 8
N_EXPERTS = N_GROUPS * EXPERTS_PER_GROUP
D_EXPERT = 256
N_SEGMENTS = 7
EPS = 1e-6
LAMBDA_INIT = 0.8 - 0.6 * math.exp(-0.3 * 0)
ALIBI_SLOPES = np.array([2.0 ** (-8.0 * (h + 1) / N_HEADS) for h in range(N_HEADS)], dtype=np.float32)
ROUTER_LANES = 128
GROUP_ID_LANE = 64
MOE_TILE = 1024
MOE_CHUNK = 128
MOE_EXPERTS_PER_STEP = 4
LOG2_E = math.log2(math.e)
IN_PROJ_STEPS = 4
IN_PROJ_SUB = 256
MIX_TILE = 512
ATTN_ROWS = 256
ATTN_KEY_TILE = 256
ATTN_VT_ROWS = HEAD_WIDTH + 16
ATTN_L_MIN = 2.0 ** -60
ATTN_L_MAX = 2.0 ** 100

F32 = jnp.float32
BF16 = jnp.bfloat16
VMEM_LIMIT_BYTES = 56 * 1024 * 1024


def _params(semantics):
    return pltpu.CompilerParams(dimension_semantics=semantics, vmem_limit_bytes=VMEM_LIMIT_BYTES)


def _dot(a, b):
    return jnp.dot(a, b, preferred_element_type=F32)


def _sigmoid(x):
    return 1.0 / (1.0 + jnp.exp(-x))


def _gelu_tanh(x):
    return 0.5 * x * (1.0 + jnp.tanh(math.sqrt(2.0 / math.pi) * (x + 0.044715 * (x * x * x))))


def _ada_kernel(c_ref, w_ref, b_ref, lq1_ref, lk1_ref, lq2_ref, lk2_ref, mod_ref, lam_ref):
    c = c_ref[...]
    act = c * _sigmoid(c)
    mod_ref[...] = jnp.dot(act, w_ref[...], preferred_element_type=F32,
                           precision=lax.Precision.HIGHEST) + b_ref[...]
    d1 = jnp.sum(lq1_ref[...] * lk1_ref[...], axis=-1, keepdims=True)
    d2 = jnp.sum(lq2_ref[...] * lk2_ref[...], axis=-1, keepdims=True)
    lam_ref[...] = jnp.exp(d1) - jnp.exp(d2) + LAMBDA_INIT


def _ada(c, w_ada, b_ada, lq1, lk1, lq2, lk2):
    batch, d = c.shape
    n = w_ada.shape[1]
    tn = 1024
    vec = pl.BlockSpec((1, HEAD_DIM), lambda j: (0, 0))
    return pl.pallas_call(
        _ada_kernel,
        grid=(n // tn,),
        in_specs=[pl.BlockSpec((batch, d), lambda j: (0, 0)),
                  pl.BlockSpec((d, tn), lambda j: (0, j)),
                  pl.BlockSpec((1, tn), lambda j: (0, j)),
                  vec, vec, vec, vec],
        out_specs=[pl.BlockSpec((batch, tn), lambda j: (0, j)),
                   pl.BlockSpec((1, 1), lambda j: (0, 0))],
        out_shape=[jax.ShapeDtypeStruct((batch, n), F32), jax.ShapeDtypeStruct((1, 1), F32)],
        compiler_params=_params(("arbitrary",)),
        name="ada",
    )(c, w_ada, b_ada.reshape(1, n), lq1.reshape(1, -1), lk1.reshape(1, -1),
      lq2.reshape(1, -1), lk2.reshape(1, -1))


def _in_proj_kernel(x_ref, mod_ref, g_ref, wa_ref, wb_ref, oa_ref, ob_ref, h_scr):
    j = pl.program_id(1)

    @pl.when(j == 0)
    def _():
        x = x_ref[...]
        y = x * lax.rsqrt(jnp.mean(x * x, axis=-1, keepdims=True) + EPS) * g_ref[...]
        h_scr[...] = (y * (1.0 + mod_ref[0, 1:2, :]) + mod_ref[0, 0:1, :]).astype(BF16)

    def project(epilogue_a, epilogue_b):
        for n in range(wa_ref.shape[1] // IN_PROJ_SUB):
            cs = slice(n * IN_PROJ_SUB, (n + 1) * IN_PROJ_SUB)
            oa_ref[0, :, cs] = epilogue_a(_dot(h_scr[...], wa_ref[:, cs])).astype(BF16)
            if epilogue_b is not None:
                ob_ref[0, :, cs] = epilogue_b(_dot(h_scr[...], wb_ref[:, cs])).astype(BF16)

    @pl.when(j == 0)
    def _():
        project(lambda acc: acc * (HEAD_DIM ** -0.5 * LOG2_E), _gelu_tanh)

    @pl.when(j == 1)
    def _():
        project(lambda acc: acc, _gelu_tanh)

    @pl.when(j == 2)
    def _():
        project(lambda acc: acc, _sigmoid)

    @pl.when(j == 3)
    def _():
        project(_sigmoid, None)


def _in_proj(x2d, mod3, norm_g, w_in_bf, seq):
    t, d = x2d.shape
    tm = 1024
    tiles_per_batch = seq // tm
    n_b = N_SEGMENTS - IN_PROJ_STEPS
    seg_a = lambda j: jnp.where(j < n_b, j, N_SEGMENTS - 1)
    seg_b = lambda j: n_b + jnp.minimum(j, n_b - 1)
    return pl.pallas_call(
        _in_proj_kernel,
        grid=(t // tm, IN_PROJ_STEPS),
        in_specs=[pl.BlockSpec((tm, d), lambda i, j: (i, 0)),
                  pl.BlockSpec((1, 6, d), lambda i, j: (i // tiles_per_batch, 0, 0)),
                  pl.BlockSpec((1, d), lambda i, j: (0, 0)),
                  pl.BlockSpec((d, d), lambda i, j: (0, seg_a(j))),
                  pl.BlockSpec((d, d), lambda i, j: (0, seg_b(j)))],
        out_specs=[pl.BlockSpec((1, tm, d), lambda i, j: (j, i, 0)),
                   pl.BlockSpec((1, tm, d), lambda i, j: (jnp.minimum(j, n_b - 1), i, 0))],
        out_shape=[jax.ShapeDtypeStruct((IN_PROJ_STEPS, t, d), BF16), jax.ShapeDtypeStruct((n_b, t, d), BF16)],
        scratch_shapes=[pltpu.VMEM((tm, d), BF16)],
        compiler_params=_params(("arbitrary", "arbitrary")),
        name="in_proj",
    )(x2d, mod3, norm_g.reshape(1, d), w_in_bf, w_in_bf)


def _attn_kernel(slopes_ref, lam_ref, q_ref, k_ref, v_ref, subg_ref, subg_col_ref, o_ref,
                 bias_scr, p_scr, vt_scr, *, rows, seq):
    h = pl.program_id(0)
    b = pl.program_id(1)
    nq = seq // rows
    contract_last = (((1,), (1,)), ((), ()))

    @pl.when(b == 0)
    def _():
        slope = -LOG2_E * slopes_ref[h]
        for qb in range(nq):
            kpos = lax.broadcasted_iota(jnp.int32, (seq, rows), 0)
            qpos = qb * rows + lax.broadcasted_iota(jnp.int32, (seq, rows), 1)
            bias_scr[qb] = slope * jnp.abs(qpos - kpos).astype(F32)
        sub = lax.broadcasted_iota(jnp.int32, (ATTN_VT_ROWS - HEAD_WIDTH, seq), 0)
        vt_scr[HEAD_WIDTH:, :] = jnp.where(sub == 0, 1.0, 0.0).astype(BF16)

    vt_scr[:HEAD_WIDTH, :] = v_ref[0].astype(F32).T.astype(BF16)
    lam = lam_ref[0, 0]

    def masked_q(r0):
        q = q_ref[0, pl.ds(r0, rows), :]
        lane = lax.broadcasted_iota(jnp.int32, q.shape, 1)
        zero = jnp.zeros_like(q)
        return jnp.where(lane < HEAD_DIM, q, zero), jnp.where(lane >= HEAD_DIM, q, zero)

    def fast_block(qb, n_bad):
        r0 = pl.multiple_of(qb * rows, rows)
        q0, q1 = masked_q(r0)
        qq = jnp.concatenate([q0, q1], axis=0)
        for c in range(seq // ATTN_KEY_TILE):
            ks = slice(c * ATTN_KEY_TILE, (c + 1) * ATTN_KEY_TILE)
            st = lax.dot_general(k_ref[0, ks, :], qq, contract_last, preferred_element_type=F32)
            bias = bias_scr[qb, ks, :]
            p_scr[ks, :] = jnp.exp2(st + jnp.concatenate([bias, bias], axis=1)).astype(BF16)
        oo = _dot(vt_scr[...], p_scr[...])
        l0 = oo[HEAD_WIDTH:HEAD_WIDTH + 1, :rows]
        l1 = oo[HEAD_WIDTH:HEAD_WIDTH + 1, rows:]
        ot = oo[:HEAD_WIDTH, :rows] / l0 - lam * (oo[:HEAD_WIDTH, rows:] / l1)
        ot = ot * lax.rsqrt(jnp.mean(ot * ot, axis=0, keepdims=True) + EPS)
        ot = ot * subg_col_ref[...] * (1.0 - LAMBDA_INIT)
        o_ref[pl.ds(r0, rows), :] = ot.T.astype(BF16)
        ok = ((l0 >= ATTN_L_MIN) & (l0 <= ATTN_L_MAX)) & ((l1 >= ATTN_L_MIN) & (l1 <= ATTN_L_MAX))
        return n_bad + jnp.where(ok, 0.0, 1.0)

    n_bad = lax.fori_loop(0, nq, fast_block, jnp.zeros((1, rows), F32), unroll=True)

    @pl.when(jnp.sum(n_bad) > 0.0)
    def _():
        subg = subg_ref[...]

        def safe_block(qb, carry):
            r0 = pl.multiple_of(qb * rows, rows)
            outs = []
            for qm in masked_q(r0):
                s = lax.dot_general(qm, k_ref[0], contract_last, preferred_element_type=F32) + bias_scr[qb].T
                p = jnp.exp2(s - jnp.max(s, axis=-1, keepdims=True))
                outs.append(_dot(p.astype(BF16), v_ref[0]) / jnp.sum(p, axis=-1, keepdims=True))
            o = outs[0] - lam * outs[1]
            o = o * lax.rsqrt(jnp.mean(o * o, axis=-1, keepdims=True) + EPS)
            o_ref[pl.ds(r0, rows), :] = (o * subg * (1.0 - LAMBDA_INIT)).astype(BF16)
            return carry

        lax.fori_loop(0, nq, safe_block, 0)


def _attention(qkv, lam, subln_g, batch, seq):
    t = batch * seq
    rows = ATTN_ROWS
    kernel = functools.partial(_attn_kernel, rows=rows, seq=seq)
    smem = pl.BlockSpec(memory_space=pltpu.SMEM)
    seg = lambda k: pl.BlockSpec((1, seq, HEAD_WIDTH), lambda h, b, k=k: (k, b, h))
    return pl.pallas_call(
        kernel,
        grid=(N_HEADS, batch),
        in_specs=[smem, smem, seg(0), seg(1), seg(2),
                  pl.BlockSpec((1, HEAD_WIDTH), lambda h, b: (0, 0)),
                  pl.BlockSpec((HEAD_WIDTH, 1), lambda h, b: (0, 0))],
        out_specs=pl.BlockSpec((seq, HEAD_WIDTH), lambda h, b: (b, h)),
        out_shape=jax.ShapeDtypeStruct((t, N_HEADS * HEAD_WIDTH), BF16),
        scratch_shapes=[pltpu.VMEM((seq // rows, seq, rows), F32),
                        pltpu.VMEM((seq, 2 * rows), BF16),
                        pltpu.VMEM((ATTN_VT_ROWS, seq), BF16)],
        compiler_params=_params(("arbitrary", "arbitrary")),
        name="diff_attn",
    )(jnp.asarray(ALIBI_SLOPES), lam, qkv, qkv, qkv, subln_g.reshape(1, HEAD_WIDTH),
      subln_g.reshape(HEAD_WIDTH, 1))


def _mix_kernel(attn_ref, u_ref, s_ref, ga_ref, gb_ref, x_ref, mod_ref,
                wap_ref, lng_ref, lnb_ref, ws_ref, bs_ref, wsp_ref, wout_ref,
                n2g_ref, wr_ref, br_ref,
                x1_ref, h2_ref, logit_ref, gated_scr):
    tm = x_ref.shape[0]
    y_attn = _dot(attn_ref[...], wap_ref[...])

    s = s_ref[0].astype(F32)
    mu = jnp.mean(s, axis=-1, keepdims=True)
    sc = s - mu
    var = jnp.mean(sc * sc, axis=-1, keepdims=True)
    v = ((sc * lax.rsqrt(var + EPS)) * lng_ref[...] + lnb_ref[...]).astype(BF16)
    n_chunks = tm // SGU_CHUNK
    for g in range(N_SGU_GROUPS):
        cols = slice(g * SGU_CHUNK, (g + 1) * SGU_CHUNK)
        v_g = jnp.concatenate([v[c * SGU_CHUNK:(c + 1) * SGU_CHUNK, cols] for c in range(n_chunks)], axis=1)
        mixed_g = _dot(ws_ref[g], v_g)
        for c in range(n_chunks):
            rows = slice(c * SGU_CHUNK, (c + 1) * SGU_CHUNK)
            mixed = mixed_g[:, c * SGU_CHUNK:(c + 1) * SGU_CHUNK] + bs_ref[g]
            gated_scr[rows, cols] = (u_ref[0, rows, cols].astype(F32) * mixed).astype(BF16)
    y_sgu = _dot(gated_scr[...], wsp_ref[...])

    y = ga_ref[0].astype(F32) * y_attn + gb_ref[0].astype(F32) * y_sgu
    x1 = x_ref[...] + mod_ref[0, 2:3, :] * _dot(y.astype(BF16), wout_ref[...])
    x1_ref[...] = x1

    h2 = x1 * lax.rsqrt(jnp.mean(x1 * x1, axis=-1, keepdims=True) + EPS) * n2g_ref[...]
    h2 = h2 * (1.0 + mod_ref[0, 4:5, :]) + mod_ref[0, 3:4, :]
    h2_hi = h2.astype(BF16)
    h2_ref[...] = h2_hi
    h2_lo = (h2 - h2_hi.astype(F32)).astype(BF16)
    parts = _dot(jnp.concatenate([h2_hi, h2_lo], axis=0), wr_ref[...])
    logit_ref[...] = ((parts[:tm, :ROUTER_LANES] + parts[:tm, ROUTER_LANES:])
                      + (parts[tm:, :ROUTER_LANES] + parts[tm:, ROUTER_LANES:]) + br_ref[...])


def _mix(attn, proj_a, proj_b, x2d, mod3, wap, lng, lnb, ws, bs_full, wsp, wout, n2g, wr, br, seq):
    t, d = x2d.shape
    tm = MIX_TILE
    tiles_per_batch = seq // tm
    const2 = lambda shape: pl.BlockSpec(shape, lambda i: (0, 0))
    const3 = lambda shape: pl.BlockSpec(shape, lambda i: (0, 0, 0))
    seg = lambda k: pl.BlockSpec((1, tm, d), lambda i, k=k: (k, i, 0))
    row = pl.BlockSpec((tm, d), lambda i: (i, 0))
    return pl.pallas_call(
        _mix_kernel,
        grid=(t // tm,),
        in_specs=[row, seg(0), seg(1), seg(2), seg(IN_PROJ_STEPS - 1), row,
                  pl.BlockSpec((1, 6, d), lambda i: (i // tiles_per_batch, 0, 0)),
                  const2((d, d)), const2((1, d)), const2((1, d)),
                  const3((N_SGU_GROUPS, SGU_CHUNK, SGU_CHUNK)),
                  const3((N_SGU_GROUPS, SGU_CHUNK, SGU_CHUNK)),
                  const2((d, d)), const2((d, d)), const2((1, d)),
                  const2((d, 2 * ROUTER_LANES)), const2((1, ROUTER_LANES))],
        out_specs=[row, row, pl.BlockSpec((tm, ROUTER_LANES), lambda i: (i, 0))],
        out_shape=[jax.ShapeDtypeStruct((t, d), F32), jax.ShapeDtypeStruct((t, d), BF16),
                   jax.ShapeDtypeStruct((t, ROUTER_LANES), F32)],
        scratch_shapes=[pltpu.VMEM((tm, d), BF16)],
        compiler_params=_params(("arbitrary",)),
        name="mix",
    )(attn, proj_b, proj_b, proj_b, proj_a, x2d, mod3, wap, lng, lnb, ws, bs_full, wsp, wout, n2g, wr, br)


def _route_kernel(logit_ref, comb_ref, combt_ref):
    z = logit_ref[...]
    lane = lax.broadcasted_iota(jnp.int32, z.shape, 1)
    neg = jnp.float32(-jnp.inf)
    big = jnp.int32(ROUTER_LANES)

    def first_argmax(val, vmax):
        return jnp.min(jnp.where(val == vmax, lane, big), axis=-1, keepdims=True)

    is_group = (lane >= N_EXPERTS) & (lane < N_EXPERTS + N_GROUPS)
    gl = jnp.where(is_group, z, neg)
    gmax = jnp.max(gl, axis=-1, keepdims=True)
    ge = jnp.exp(gl - gmax)
    gp = ge / jnp.sum(ge, axis=-1, keepdims=True)
    gval = jnp.max(gp, axis=-1, keepdims=True)
    gidx = first_argmax(gp, gval) - N_EXPERTS

    in_group = (lane >= gidx * EXPERTS_PER_GROUP) & (lane < (gidx + 1) * EXPERTS_PER_GROUP)
    el = jnp.where(in_group, z, neg)
    emax = jnp.max(el, axis=-1, keepdims=True)
    ee = jnp.exp(el - emax)
    ep = ee / jnp.sum(ee, axis=-1, keepdims=True)
    ep = jnp.where(in_group, ep, -1.0)
    ev0 = jnp.max(ep, axis=-1, keepdims=True)
    ei0 = first_argmax(ep, ev0)
    ep_rest = jnp.where(lane == ei0, -1.0, ep)
    ev1 = jnp.max(ep_rest, axis=-1, keepdims=True)
    ei1 = first_argmax(ep_rest, ev1)
    denom = ev0 + ev1
    comb = jnp.where(lane == ei0, ev0 / denom * gval,
                     jnp.where(lane == ei1, ev1 / denom * gval,
                               jnp.where(lane == GROUP_ID_LANE, gidx.astype(F32), 0.0)))
    comb_ref[...] = comb
    combt_ref[...] = comb.T


def _route(logits):
    t = logits.shape[0]
    tm = 1024
    spec = pl.BlockSpec((tm, ROUTER_LANES), lambda i: (i, 0))
    return pl.pallas_call(
        _route_kernel, grid=(t // tm,), in_specs=[spec],
        out_specs=[spec, pl.BlockSpec((ROUTER_LANES, tm), lambda i: (0, i))],
        out_shape=[jax.ShapeDtypeStruct((t, ROUTER_LANES), F32), jax.ShapeDtypeStruct((ROUTER_LANES, t), F32)],
        compiler_params=_params(("arbitrary",)), name="route",
    )(logits)


def _moe_kernel(h2_ref, comb_ref, combt_ref, wgu_ref, wd_ref, x1_ref, mod_ref, fg_ref, out_ref,
                onehot_scr, xs_scr, wl_scr, outc_scr, meta_ref):
    j = pl.program_id(1)
    tt = h2_ref.shape[0]
    rc = onehot_scr.shape[0]
    steps_per_group = EXPERTS_PER_GROUP // MOE_EXPERTS_PER_STEP
    g = j // steps_per_group

    @pl.when(j == 0)
    def _():
        gid = combt_ref[GROUP_ID_LANE:GROUP_ID_LANE + 1, :]
        sub = lax.broadcasted_iota(jnp.int32, (8, tt), 0).astype(F32)
        member = jnp.where(gid == sub, 1.0, 0.0)
        before = jnp.where(lax.broadcasted_iota(jnp.int32, (tt, tt), 0)
                           < lax.broadcasted_iota(jnp.int32, (tt, tt), 1), 1.0, 0.0).astype(BF16)
        rank = _dot(member.astype(BF16), before)
        pos = jnp.zeros((1, tt), F32)
        off = jnp.int32(0)
        for gg in range(N_GROUPS):
            count = jnp.sum(member[gg:gg + 1, :]).astype(jnp.int32)
            n_chunks = (count + (MOE_CHUNK - 1)) // MOE_CHUNK
            meta_ref[gg] = off
            meta_ref[N_GROUPS + gg] = n_chunks
            pos = pos + member[gg:gg + 1, :] * (off.astype(F32) + rank[gg:gg + 1, :])
            off = off + n_chunks * MOE_CHUNK
        row = lax.broadcasted_iota(jnp.int32, (rc, tt), 0).astype(F32)
        onehot = jnp.where(row == pos, 1.0, 0.0).astype(BF16)
        onehot_scr[...] = onehot
        xs_scr[...] = _dot(onehot, h2_ref[...]).astype(BF16)
        comb = comb_ref[...]
        c_hi = comb.astype(BF16)
        c_lo = (comb - c_hi.astype(F32)).astype(BF16)
        wl2 = _dot(onehot, jnp.concatenate([c_hi, c_lo], axis=1))
        wl_scr[...] = wl2[:, :ROUTER_LANES] + wl2[:, ROUTER_LANES:]
        outc_scr[...] = jnp.zeros_like(outc_scr)

    off_g = meta_ref[g]
    n_chunks_g = meta_ref[N_GROUPS + g]

    def experts(r0, rows):
        xs = xs_scr[pl.ds(r0, rows), :]
        wl = wl_scr[pl.ds(r0, rows), :]
        lane = lax.broadcasted_iota(jnp.int32, wl.shape, 1)
        acts = []
        for e in range(MOE_EXPERTS_PER_STEP):
            gu = _dot(xs, wgu_ref[e])
            gate = gu[:, :D_EXPERT]
            up = gu[:, D_EXPERT:]
            w = jnp.sum(jnp.where(lane == j * MOE_EXPERTS_PER_STEP + e, wl, 0.0), axis=-1, keepdims=True)
            acts.append((gate * _sigmoid(gate) * up * w).astype(BF16))
        wd_all = wd_ref[...].reshape(MOE_EXPERTS_PER_STEP * D_EXPERT, D_MODEL)
        outc_scr[pl.ds(r0, rows), :] += _dot(jnp.concatenate(acts, axis=1), wd_all)

    def chunk_pair(k, carry):
        experts(pl.multiple_of(off_g + k * (2 * MOE_CHUNK), MOE_CHUNK), 2 * MOE_CHUNK)
        return carry

    lax.fori_loop(0, n_chunks_g // 2, chunk_pair, 0)

    @pl.when(n_chunks_g % 2 == 1)
    def _():
        experts(pl.multiple_of(off_g + (n_chunks_g - 1) * MOE_CHUNK, MOE_CHUNK), MOE_CHUNK)

    @pl.when(j == pl.num_programs(1) - 1)
    def _():
        y = lax.dot_general(onehot_scr[...], outc_scr[...].astype(BF16), (((0,), (0,)), ((), ())),
                            preferred_element_type=F32)
        x2 = x1_ref[...] + mod_ref[0, 5:6, :] * y
        out_ref[...] = x2 * lax.rsqrt(jnp.mean(x2 * x2, axis=-1, keepdims=True) + EPS) * fg_ref[...]


def _moe(h2, comb, combt, wgu, wd, x1, mod3, final_g, seq):
    t, d = x1.shape
    tm = MOE_TILE
    tiles_per_batch = seq // tm
    n_steps = N_EXPERTS // MOE_EXPERTS_PER_STEP
    compact_rows = tm + N_GROUPS * MOE_CHUNK
    row = pl.BlockSpec((tm, d), lambda i, j: (i, 0))
    return pl.pallas_call(
        _moe_kernel,
        grid=(t // tm, n_steps),
        in_specs=[row,
                  pl.BlockSpec((tm, ROUTER_LANES), lambda i, j: (i, 0)),
                  pl.BlockSpec((ROUTER_LANES, tm), lambda i, j: (0, i)),
                  pl.BlockSpec((MOE_EXPERTS_PER_STEP, d, 2 * D_EXPERT), lambda i, j: (j, 0, 0)),
                  pl.BlockSpec((MOE_EXPERTS_PER_STEP, D_EXPERT, d), lambda i, j: (j, 0, 0)),
                  pl.BlockSpec((tm, d), lambda i, j: (i, 0), pipeline_mode=pl.Buffered(1)),
                  pl.BlockSpec((1, 6, d), lambda i, j: (i // tiles_per_batch, 0, 0)),
                  pl.BlockSpec((1, d), lambda i, j: (0, 0))],
        out_specs=row,
        out_shape=jax.ShapeDtypeStruct((t, d), F32),
        scratch_shapes=[pltpu.VMEM((compact_rows, tm), BF16),
                        pltpu.VMEM((compact_rows, d), BF16),
                        pltpu.VMEM((compact_rows, ROUTER_LANES), F32),
                        pltpu.VMEM((compact_rows, d), F32),
                        pltpu.SMEM((2 * N_GROUPS,), jnp.int32)],
        compiler_params=_params(("arbitrary", "arbitrary")),
        name="moe",
    )(h2, comb, combt, wgu, wd, x1, mod3, final_g.reshape(1, d))


def _router_weights(w_rg, b_rg, w_re, b_re):
    d = w_rg.shape[0]
    pad = ROUTER_LANES - N_EXPERTS - N_GROUPS
    w = jnp.concatenate([w_re, w_rg, jnp.zeros((d, pad), F32)], axis=1)
    b = jnp.concatenate([b_re, b_rg, jnp.zeros((pad,), F32)]).reshape(1, ROUTER_LANES)
    w_hi = w.astype(BF16)
    w_lo = (w - w_hi.astype(F32)).astype(BF16)
    return jnp.concatenate([w_hi, w_lo], axis=1), b


def kernel(x, c, w_ada, b_ada, norm1_g, w_in, lambda_q1, lambda_k1, lambda_q2, lambda_k2, subln_g, w_attn_proj, sgu_ln_g, sgu_ln_b, sgu_w_s, sgu_b_s, w_sgu_proj, w_out, norm2_g, w_router_group, b_router_group, w_router_expert, b_router_expert, w_expert_gate_up, w_expert_down, final_g):
    batch, seq, d = x.shape
    assert w_ada.shape[0] == 1, "single-layer trunk"
    x2d = x.reshape(batch * seq, d)

    mod, lam = _ada(c, w_ada[0], b_ada[0], lambda_q1[0], lambda_k1[0], lambda_q2[0], lambda_k2[0])
    mod3 = mod.reshape(batch, 6, d)

    proj_a, proj_b = _in_proj(x2d, mod3, norm1_g[0], w_in[0].astype(BF16), seq)
    attn = _attention(proj_a, lam, subln_g[0], batch, seq)

    bs_full = jnp.broadcast_to(sgu_b_s[0][:, :, None], (N_SGU_GROUPS, SGU_CHUNK, SGU_CHUNK))
    wr, br = _router_weights(w_router_group[0], b_router_group[0], w_router_expert[0], b_router_expert[0])
    x1, h2, logits = _mix(attn, proj_a, proj_b, x2d, mod3, w_attn_proj[0].astype(BF16),
                          sgu_ln_g[0].reshape(1, d), sgu_ln_b[0].reshape(1, d),
                          sgu_w_s[0].astype(BF16), bs_full, w_sgu_proj[0].astype(BF16),
                          w_out[0].astype(BF16), norm2_g[0].reshape(1, d), wr, br, seq)
    comb, combt = _route(logits)
    out = _moe(h2, comb, combt, w_expert_gate_up[0].astype(BF16), w_expert_down[0].astype(BF16),
               x1, mod3, final_g, seq)
    return out.reshape(batch, seq, d)
```

```python
import functools
import math

import jax
import jax.numpy as jnp
import numpy as np
from jax import lax
from jax.experimental import pallas as pl
from jax.experimental.pallas import tpu as pltpu

D_MODEL = 1024
N_HEADS = 8
HEAD_DIM = 64
HEAD_WIDTH = 2 * HEAD_DIM
N_SGU_GROUPS = 8
SGU_CHUNK = 128
N_GROUPS = 4
EXPERTS_PER_GROUP = 8
N_EXPERTS = N_GROUPS * EXPERTS_PER_GROUP
D_EXPERT = 256
N_SEGMENTS = 7
EPS = 1e-6
LAMBDA_INIT = 0.8 - 0.6 * math.exp(-0.3 * 0)
ALIBI_SLOPES = np.array([2.0 ** (-8.0 * (h + 1) / N_HEADS) for h in range(N_HEADS)], dtype=np.float32)
ROUTER_LANES = 128
GROUP_ID_LANE = 64
MOE_TILE = 1024
MOE_CHUNK = 128
MOE_ROW_ALIGN = 16
MOE_EXPERTS_PER_STEP = 4
LOG2_E = math.log2(math.e)
IN_PROJ_TILE = 1024
IN_PROJ_STEPS = 4
IN_PROJ_SUB = 256
MIX_TILE = 512
ATTN_ROWS = 256
ATTN_KEY_TILE = 1024
ATTN_VT_ROWS = HEAD_WIDTH + 16
ATTN_L_MIN = 2.0 ** -60
ATTN_L_MAX = 2.0 ** 100

F32 = jnp.float32
BF16 = jnp.bfloat16
VMEM_LIMIT_BYTES = 56 * 1024 * 1024


def _params(semantics):
    return pltpu.CompilerParams(dimension_semantics=semantics, vmem_limit_bytes=VMEM_LIMIT_BYTES)


def _dot(a, b):
    return jnp.dot(a, b, preferred_element_type=F32)


def _sigmoid(x):
    return 1.0 / (1.0 + jnp.exp(-x))


def _gelu_tanh(x):
    return 0.5 * x * (1.0 + jnp.tanh(math.sqrt(2.0 / math.pi) * (x + 0.044715 * (x * x * x))))


def _ada_kernel(c_ref, w_ref, b_ref, lq1_ref, lk1_ref, lq2_ref, lk2_ref, mod_ref, lam_ref):
    c = c_ref[...]
    act = c * _sigmoid(c)
    mod_ref[...] = jnp.dot(act, w_ref[...], preferred_element_type=F32,
                           precision=lax.Precision.HIGHEST) + b_ref[...]
    d1 = jnp.sum(lq1_ref[...] * lk1_ref[...], axis=-1, keepdims=True)
    d2 = jnp.sum(lq2_ref[...] * lk2_ref[...], axis=-1, keepdims=True)
    lam_ref[...] = jnp.exp(d1) - jnp.exp(d2) + LAMBDA_INIT


def _ada(c, w_ada, b_ada, lq1, lk1, lq2, lk2):
    batch, d = c.shape
    n = w_ada.shape[1]
    tn = 1024
    vec = pl.BlockSpec((1, HEAD_DIM), lambda j: (0, 0))
    return pl.pallas_call(
        _ada_kernel,
        grid=(n // tn,),
        in_specs=[pl.BlockSpec((batch, d), lambda j: (0, 0)),
                  pl.BlockSpec((d, tn), lambda j: (0, j)),
                  pl.BlockSpec((1, tn), lambda j: (0, j)),
                  vec, vec, vec, vec],
        out_specs=[pl.BlockSpec((batch, tn), lambda j: (0, j)),
                   pl.BlockSpec((1, 1), lambda j: (0, 0))],
        out_shape=[jax.ShapeDtypeStruct((batch, n), F32), jax.ShapeDtypeStruct((1, 1), F32)],
        compiler_params=_params(("arbitrary",)),
        name="ada",
    )(c, w_ada, b_ada.reshape(1, n), lq1.reshape(1, -1), lk1.reshape(1, -1),
      lq2.reshape(1, -1), lk2.reshape(1, -1))


def _in_proj_kernel(x_ref, mod_ref, g_ref, wa_ref, wb_ref, oa_ref, ob_ref, h_scr):
    j = pl.program_id(1)

    @pl.when(j == 0)
    def _():
        x = x_ref[...]
        y = x * lax.rsqrt(jnp.mean(x * x, axis=-1, keepdims=True) + EPS) * g_ref[...]
        h_scr[...] = (y * (1.0 + mod_ref[0, 1:2, :]) + mod_ref[0, 0:1, :]).astype(BF16)

    def project(epilogue_a, epilogue_b):
        for n in range(wa_ref.shape[1] // IN_PROJ_SUB):
            cs = slice(n * IN_PROJ_SUB, (n + 1) * IN_PROJ_SUB)
            oa_ref[0, :, cs] = epilogue_a(_dot(h_scr[...], wa_ref[:, cs])).astype(BF16)
            if epilogue_b is not None:
                ob_ref[0, :, cs] = epilogue_b(_dot(h_scr[...], wb_ref[:, cs])).astype(BF16)

    @pl.when(j == 0)
    def _():
        project(lambda acc: acc * (HEAD_DIM ** -0.5 * LOG2_E), _gelu_tanh)

    @pl.when(j == 1)
    def _():
        project(lambda acc: acc, _gelu_tanh)

    @pl.when(j == 2)
    def _():
        project(lambda acc: acc, _sigmoid)

    @pl.when(j == 3)
    def _():
        project(_sigmoid, None)


def _in_proj(x2d, mod3, norm_g, w_in_bf, seq):
    t, d = x2d.shape
    tm = IN_PROJ_TILE
    tiles_per_batch = seq // tm
    n_b = N_SEGMENTS - IN_PROJ_STEPS
    seg_a = lambda j: jnp.where(j < n_b, j, N_SEGMENTS - 1)
    seg_b = lambda j: n_b + jnp.minimum(j, n_b - 1)
    return pl.pallas_call(
        _in_proj_kernel,
        grid=(t // tm, IN_PROJ_STEPS),
        in_specs=[pl.BlockSpec((tm, d), lambda i, j: (i, 0)),
                  pl.BlockSpec((1, 6, d), lambda i, j: (i // tiles_per_batch, 0, 0)),
                  pl.BlockSpec((1, d), lambda i, j: (0, 0)),
                  pl.BlockSpec((d, d), lambda i, j: (0, seg_a(j))),
                  pl.BlockSpec((d, d), lambda i, j: (0, seg_b(j)))],
        out_specs=[pl.BlockSpec((1, tm, d), lambda i, j: (j, i, 0)),
                   pl.BlockSpec((1, tm, d), lambda i, j: (jnp.minimum(j, n_b - 1), i, 0))],
        out_shape=[jax.ShapeDtypeStruct((IN_PROJ_STEPS, t, d), BF16), jax.ShapeDtypeStruct((n_b, t, d), BF16)],
        scratch_shapes=[pltpu.VMEM((tm, d), BF16)],
        compiler_params=_params(("arbitrary", "arbitrary")),
        name="in_proj",
    )(x2d, mod3, norm_g.reshape(1, d), w_in_bf, w_in_bf)


def _attn_kernel(slopes_ref, lam_ref, q_ref, k_ref, v_ref, subg_ref, subg_col_ref, o_ref,
                 bias_scr, p_scr, vt_scr, *, rows, seq):
    h = pl.program_id(0)
    b = pl.program_id(1)
    nq = seq // rows
    contract_last = (((1,), (1,)), ((), ()))

    @pl.when(b == 0)
    def _():
        slope = -LOG2_E * slopes_ref[h]
        for qb in range(nq):
            kpos = lax.broadcasted_iota(jnp.int32, (seq, rows), 0)
            qpos = qb * rows + lax.broadcasted_iota(jnp.int32, (seq, rows), 1)
            bias_scr[qb] = slope * jnp.abs(qpos - kpos).astype(F32)
        sub = lax.broadcasted_iota(jnp.int32, (ATTN_VT_ROWS - HEAD_WIDTH, seq), 0)
        vt_scr[HEAD_WIDTH:, :] = jnp.where(sub == 0, 1.0, 0.0).astype(BF16)

    vt_scr[:HEAD_WIDTH, :] = v_ref[0].astype(F32).T.astype(BF16)
    lam = lam_ref[0, 0]

    def masked_q(r0):
        q = q_ref[0, pl.ds(r0, rows), :]
        lane = lax.broadcasted_iota(jnp.int32, q.shape, 1)
        zero = jnp.zeros_like(q)
        return jnp.where(lane < HEAD_DIM, q, zero), jnp.where(lane >= HEAD_DIM, q, zero)

    def fast_block(qb, n_bad):
        r0 = pl.multiple_of(qb * rows, rows)
        q0, q1 = masked_q(r0)
        qq = jnp.concatenate([q0, q1], axis=0)
        for c in range(seq // ATTN_KEY_TILE):
            ks = slice(c * ATTN_KEY_TILE, (c + 1) * ATTN_KEY_TILE)
            st = lax.dot_general(k_ref[0, ks, :], qq, contract_last, preferred_element_type=F32)
            bias = bias_scr[qb, ks, :]
            p_scr[ks, :] = jnp.exp2(st + jnp.concatenate([bias, bias], axis=1)).astype(BF16)
        oo = _dot(vt_scr[...], p_scr[...])
        l0 = oo[HEAD_WIDTH:HEAD_WIDTH + 1, :rows]
        l1 = oo[HEAD_WIDTH:HEAD_WIDTH + 1, rows:]
        ot = oo[:HEAD_WIDTH, :rows] / l0 - lam * (oo[:HEAD_WIDTH, rows:] / l1)
        ot = ot * lax.rsqrt(jnp.mean(ot * ot, axis=0, keepdims=True) + EPS)
        ot = ot * subg_col_ref[...] * (1.0 - LAMBDA_INIT)
        o_ref[pl.ds(r0, rows), :] = ot.T.astype(BF16)
        ok = ((l0 >= ATTN_L_MIN) & (l0 <= ATTN_L_MAX)) & ((l1 >= ATTN_L_MIN) & (l1 <= ATTN_L_MAX))
        return n_bad + jnp.where(ok, 0.0, 1.0)

    n_bad = lax.fori_loop(0, nq, fast_block, jnp.zeros((1, rows), F32), unroll=True)

    @pl.when(jnp.sum(n_bad) > 0.0)
    def _():
        subg = subg_ref[...]

        def safe_block(qb, carry):
            r0 = pl.multiple_of(qb * rows, rows)
            outs = []
            for qm in masked_q(r0):
                s = lax.dot_general(qm, k_ref[0], contract_last, preferred_element_type=F32) + bias_scr[qb].T
                p = jnp.exp2(s - jnp.max(s, axis=-1, keepdims=True))
                outs.append(_dot(p.astype(BF16), v_ref[0]) / jnp.sum(p, axis=-1, keepdims=True))
            o = outs[0] - lam * outs[1]
            o = o * lax.rsqrt(jnp.mean(o * o, axis=-1, keepdims=True) + EPS)
            o_ref[pl.ds(r0, rows), :] = (o * subg * (1.0 - LAMBDA_INIT)).astype(BF16)
            return carry

        lax.fori_loop(0, nq, safe_block, 0)


def _attention(qkv, lam, subln_g, batch, seq):
    t = batch * seq
    rows = ATTN_ROWS
    kernel = functools.partial(_attn_kernel, rows=rows, seq=seq)
    smem = pl.BlockSpec(memory_space=pltpu.SMEM)
    seg = lambda k: pl.BlockSpec((1, seq, HEAD_WIDTH), lambda h, b, k=k: (k, b, h))
    return pl.pallas_call(
        kernel,
        grid=(N_HEADS, batch),
        in_specs=[smem, smem, seg(0), seg(1), seg(2),
                  pl.BlockSpec((1, HEAD_WIDTH), lambda h, b: (0, 0)),
                  pl.BlockSpec((HEAD_WIDTH, 1), lambda h, b: (0, 0))],
        out_specs=pl.BlockSpec((seq, HEAD_WIDTH), lambda h, b: (b, h)),
        out_shape=jax.ShapeDtypeStruct((t, N_HEADS * HEAD_WIDTH), BF16),
        scratch_shapes=[pltpu.VMEM((seq // rows, seq, rows), F32),
                        pltpu.VMEM((seq, 2 * rows), BF16),
                        pltpu.VMEM((ATTN_VT_ROWS, seq), BF16)],
        compiler_params=_params(("arbitrary", "arbitrary")),
        name="diff_attn",
    )(jnp.asarray(ALIBI_SLOPES), lam, qkv, qkv, qkv, subln_g.reshape(1, HEAD_WIDTH),
      subln_g.reshape(HEAD_WIDTH, 1))


def _mix_kernel(attn_ref, u_ref, s_ref, ga_ref, gb_ref, x_ref, mod_ref,
                wap_ref, lng_ref, lnb_ref, ws_ref, bs_ref, wsp_ref, wout_ref,
                n2g_ref, wr_ref, br_ref,
                x1_ref, h2_ref, logit_ref, gated_scr):
    tm = x_ref.shape[0]
    y_attn = _dot(attn_ref[...], wap_ref[...])

    s = s_ref[0].astype(F32)
    mu = jnp.mean(s, axis=-1, keepdims=True)
    sc = s - mu
    var = jnp.mean(sc * sc, axis=-1, keepdims=True)
    v = ((sc * lax.rsqrt(var + EPS)) * lng_ref[...] + lnb_ref[...]).astype(BF16)
    n_chunks = tm // SGU_CHUNK
    for g in range(N_SGU_GROUPS):
        cols = slice(g * SGU_CHUNK, (g + 1) * SGU_CHUNK)
        v_g = jnp.concatenate([v[c * SGU_CHUNK:(c + 1) * SGU_CHUNK, cols] for c in range(n_chunks)], axis=1)
        mixed_g = _dot(ws_ref[g], v_g)
        for c in range(n_chunks):
            rows = slice(c * SGU_CHUNK, (c + 1) * SGU_CHUNK)
            mixed = mixed_g[:, c * SGU_CHUNK:(c + 1) * SGU_CHUNK] + bs_ref[g]
            gated_scr[rows, cols] = (u_ref[0, rows, cols].astype(F32) * mixed).astype(BF16)
    y_sgu = _dot(gated_scr[...], wsp_ref[...])

    y = ga_ref[0].astype(F32) * y_attn + gb_ref[0].astype(F32) * y_sgu
    x1 = x_ref[...] + mod_ref[0, 2:3, :] * _dot(y.astype(BF16), wout_ref[...])
    x1_ref[...] = x1

    h2 = x1 * lax.rsqrt(jnp.mean(x1 * x1, axis=-1, keepdims=True) + EPS) * n2g_ref[...]
    h2 = h2 * (1.0 + mod_ref[0, 4:5, :]) + mod_ref[0, 3:4, :]
    h2_hi = h2.astype(BF16)
    h2_ref[...] = h2_hi
    h2_lo = (h2 - h2_hi.astype(F32)).astype(BF16)
    parts = _dot(jnp.concatenate([h2_hi, h2_lo], axis=0), wr_ref[...])
    logit_ref[...] = ((parts[:tm, :ROUTER_LANES] + parts[:tm, ROUTER_LANES:])
                      + (parts[tm:, :ROUTER_LANES] + parts[tm:, ROUTER_LANES:]) + br_ref[...])


def _mix(attn, proj_a, proj_b, x2d, mod3, wap, lng, lnb, ws, bs_full, wsp, wout, n2g, wr, br, seq):
    t, d = x2d.shape
    tm = MIX_TILE
    tiles_per_batch = seq // tm
    const2 = lambda shape: pl.BlockSpec(shape, lambda i: (0, 0))
    const3 = lambda shape: pl.BlockSpec(shape, lambda i: (0, 0, 0))
    seg = lambda k: pl.BlockSpec((1, tm, d), lambda i, k=k: (k, i, 0))
    row = pl.BlockSpec((tm, d), lambda i: (i, 0))
    return pl.pallas_call(
        _mix_kernel,
        grid=(t // tm,),
        in_specs=[row, seg(0), seg(1), seg(2), seg(IN_PROJ_STEPS - 1), row,
                  pl.BlockSpec((1, 6, d), lambda i: (i // tiles_per_batch, 0, 0)),
                  const2((d, d)), const2((1, d)), const2((1, d)),
                  const3((N_SGU_GROUPS, SGU_CHUNK, SGU_CHUNK)),
                  const3((N_SGU_GROUPS, SGU_CHUNK, SGU_CHUNK)),
                  const2((d, d)), const2((d, d)), const2((1, d)),
                  const2((d, 2 * ROUTER_LANES)), const2((1, ROUTER_LANES))],
        out_specs=[row, row, pl.BlockSpec((tm, ROUTER_LANES), lambda i: (i, 0))],
        out_shape=[jax.ShapeDtypeStruct((t, d), F32), jax.ShapeDtypeStruct((t, d), BF16),
                   jax.ShapeDtypeStruct((t, ROUTER_LANES), F32)],
        scratch_shapes=[pltpu.VMEM((tm, d), BF16)],
        compiler_params=_params(("arbitrary",)),
        name="mix",
    )(attn, proj_b, proj_b, proj_b, proj_a, x2d, mod3, wap, lng, lnb, ws, bs_full, wsp, wout, n2g, wr, br)


def _route_kernel(logit_ref, comb_ref, combt_ref):
    z = logit_ref[...]
    lane = lax.broadcasted_iota(jnp.int32, z.shape, 1)
    neg = jnp.float32(-jnp.inf)
    big = jnp.int32(ROUTER_LANES)

    def first_argmax(val, vmax):
        return jnp.min(jnp.where(val == vmax, lane, big), axis=-1, keepdims=True)

    is_group = (lane >= N_EXPERTS) & (lane < N_EXPERTS + N_GROUPS)
    gl = jnp.where(is_group, z, neg)
    gmax = jnp.max(gl, axis=-1, keepdims=True)
    ge = jnp.exp(gl - gmax)
    gp = ge / jnp.sum(ge, axis=-1, keepdims=True)
    gval = jnp.max(gp, axis=-1, keepdims=True)
    gidx = first_argmax(gp, gval) - N_EXPERTS

    in_group = (lane >= gidx * EXPERTS_PER_GROUP) & (lane < (gidx + 1) * EXPERTS_PER_GROUP)
    el = jnp.where(in_group, z, neg)
    emax = jnp.max(el, axis=-1, keepdims=True)
    ee = jnp.exp(el - emax)
    ep = ee / jnp.sum(ee, axis=-1, keepdims=True)
    ep = jnp.where(in_group, ep, -1.0)
    ev0 = jnp.max(ep, axis=-1, keepdims=True)
    ei0 = first_argmax(ep, ev0)
    ep_rest = jnp.where(lane == ei0, -1.0, ep)
    ev1 = jnp.max(ep_rest, axis=-1, keepdims=True)
    ei1 = first_argmax(ep_rest, ev1)
    denom = ev0 + ev1
    comb = jnp.where(lane == ei0, ev0 / denom * gval,
                     jnp.where(lane == ei1, ev1 / denom * gval,
                               jnp.where(lane == GROUP_ID_LANE, gidx.astype(F32), 0.0)))
    comb_ref[...] = comb
    combt_ref[...] = comb.T


def _route(logits):
    t = logits.shape[0]
    tm = 1024
    spec = pl.BlockSpec((tm, ROUTER_LANES), lambda i: (i, 0))
    return pl.pallas_call(
        _route_kernel, grid=(t // tm,), in_specs=[spec],
        out_specs=[spec, pl.BlockSpec((ROUTER_LANES, tm), lambda i: (0, i))],
        out_shape=[jax.ShapeDtypeStruct((t, ROUTER_LANES), F32), jax.ShapeDtypeStruct((ROUTER_LANES, t), F32)],
        compiler_params=_params(("arbitrary",)), name="route",
    )(logits)


def _moe_kernel(h2_ref, comb_ref, combt_ref, wgu_ref, wd_ref, x1_ref, mod_ref, fg_ref, out_ref,
                onehot_scr, xs_scr, wl_scr, outc_scr, meta_ref):
    j = pl.program_id(1)
    tt = h2_ref.shape[0]
    rc = onehot_scr.shape[0]
    steps_per_group = EXPERTS_PER_GROUP // MOE_EXPERTS_PER_STEP
    g = j // steps_per_group

    @pl.when(j == 0)
    def _():
        gid = combt_ref[GROUP_ID_LANE:GROUP_ID_LANE + 1, :]
        sub = lax.broadcasted_iota(jnp.int32, (8, tt), 0).astype(F32)
        member = jnp.where(gid == sub, 1.0, 0.0)
        before = jnp.where(lax.broadcasted_iota(jnp.int32, (tt, tt), 0)
                           < lax.broadcasted_iota(jnp.int32, (tt, tt), 1), 1.0, 0.0).astype(BF16)
        rank = _dot(member.astype(BF16), before)
        pos = jnp.zeros((1, tt), F32)
        off = jnp.int32(0)
        for gg in range(N_GROUPS):
            count = jnp.sum(member[gg:gg + 1, :]).astype(jnp.int32)
            n_chunks = (count + (MOE_CHUNK - 1)) // MOE_CHUNK
            meta_ref[gg] = off
            meta_ref[N_GROUPS + gg] = n_chunks
            pos = pos + member[gg:gg + 1, :] * (off.astype(F32) + rank[gg:gg + 1, :])
            off = off + (count + (MOE_ROW_ALIGN - 1)) // MOE_ROW_ALIGN * MOE_ROW_ALIGN
        row = lax.broadcasted_iota(jnp.int32, (rc, tt), 0).astype(F32)
        onehot = jnp.where(row == pos, 1.0, 0.0).astype(BF16)
        onehot_scr[...] = onehot
        xs_scr[:rc, :] = _dot(onehot, h2_ref[...]).astype(BF16)
        xs_scr[rc:, :] = jnp.zeros((xs_scr.shape[0] - rc, xs_scr.shape[1]), BF16)
        comb = comb_ref[...]
        c_hi = comb.astype(BF16)
        c_lo = (comb - c_hi.astype(F32)).astype(BF16)
        wl2 = _dot(onehot, jnp.concatenate([c_hi, c_lo], axis=1))
        wl_scr[:rc, :] = wl2[:, :ROUTER_LANES] + wl2[:, ROUTER_LANES:]
        wl_scr[rc:, :] = jnp.zeros((wl_scr.shape[0] - rc, ROUTER_LANES), F32)
        outc_scr[...] = jnp.zeros_like(outc_scr)

    off_g = meta_ref[g]
    n_chunks_g = meta_ref[N_GROUPS + g]

    def experts(r0, rows):
        xs = xs_scr[pl.ds(r0, rows), :]
        wl = wl_scr[pl.ds(r0, rows), :]
        lane = lax.broadcasted_iota(jnp.int32, wl.shape, 1)
        acts = []
        for e in range(MOE_EXPERTS_PER_STEP):
            gu = _dot(xs, wgu_ref[e])
            gate = gu[:, :D_EXPERT]
            up = gu[:, D_EXPERT:]
            w = jnp.sum(jnp.where(lane == j * MOE_EXPERTS_PER_STEP + e, wl, 0.0), axis=-1, keepdims=True)
            acts.append((gate * _sigmoid(gate) * up * w).astype(BF16))
        wd_all = wd_ref[...].reshape(MOE_EXPERTS_PER_STEP * D_EXPERT, D_MODEL)
        outc_scr[pl.ds(r0, rows), :] += _dot(jnp.concatenate(acts, axis=1), wd_all)

    def chunk_pair(k, carry):
        experts(pl.multiple_of(off_g + k * (2 * MOE_CHUNK), MOE_ROW_ALIGN), 2 * MOE_CHUNK)
        return carry

    lax.fori_loop(0, n_chunks_g // 2, chunk_pair, 0)

    @pl.when(n_chunks_g % 2 == 1)
    def _():
        experts(pl.multiple_of(off_g + (n_chunks_g - 1) * MOE_CHUNK, MOE_ROW_ALIGN), MOE_CHUNK)

    @pl.when(j == pl.num_programs(1) - 1)
    def _():
        y = lax.dot_general(onehot_scr[...], outc_scr[:rc, :].astype(BF16), (((0,), (0,)), ((), ())),
                            preferred_element_type=F32)
        x2 = x1_ref[...] + mod_ref[0, 5:6, :] * y
        out_ref[...] = x2 * lax.rsqrt(jnp.mean(x2 * x2, axis=-1, keepdims=True) + EPS) * fg_ref[...]


def _moe(h2, comb, combt, wgu, wd, x1, mod3, final_g, seq):
    t, d = x1.shape
    tm = MOE_TILE
    tiles_per_batch = seq // tm
    n_steps = N_EXPERTS // MOE_EXPERTS_PER_STEP
    compact_rows = tm + N_GROUPS * MOE_ROW_ALIGN
    buffer_rows = compact_rows + MOE_CHUNK
    row = pl.BlockSpec((tm, d), lambda i, j: (i, 0))
    return pl.pallas_call(
        _moe_kernel,
        grid=(t // tm, n_steps),
        in_specs=[row,
                  pl.BlockSpec((tm, ROUTER_LANES), lambda i, j: (i, 0)),
                  pl.BlockSpec((ROUTER_LANES, tm), lambda i, j: (0, i)),
                  pl.BlockSpec((MOE_EXPERTS_PER_STEP, d, 2 * D_EXPERT), lambda i, j: (j, 0, 0)),
                  pl.BlockSpec((MOE_EXPERTS_PER_STEP, D_EXPERT, d), lambda i, j: (j, 0, 0)),
                  row,
                  pl.BlockSpec((1, 6, d), lambda i, j: (i // tiles_per_batch, 0, 0)),
                  pl.BlockSpec((1, d), lambda i, j: (0, 0))],
        out_specs=row,
        out_shape=jax.ShapeDtypeStruct((t, d), F32),
        scratch_shapes=[pltpu.VMEM((compact_rows, tm), BF16),
                        pltpu.VMEM((buffer_rows, d), BF16),
                        pltpu.VMEM((buffer_rows, ROUTER_LANES), F32),
                        pltpu.VMEM((buffer_rows, d), F32),
                        pltpu.SMEM((2 * N_GROUPS,), jnp.int32)],
        compiler_params=_params(("arbitrary", "arbitrary")),
        name="moe",
    )(h2, comb, combt, wgu, wd, x1, mod3, final_g.reshape(1, d))


def _router_weights(w_rg, b_rg, w_re, b_re):
    d = w_rg.shape[0]
    pad = ROUTER_LANES - N_EXPERTS - N_GROUPS
    w = jnp.concatenate([w_re, w_rg, jnp.zeros((d, pad), F32)], axis=1)
    b = jnp.concatenate([b_re, b_rg, jnp.zeros((pad,), F32)]).reshape(1, ROUTER_LANES)
    w_hi = w.astype(BF16)
    w_lo = (w - w_hi.astype(F32)).astype(BF16)
    return jnp.concatenate([w_hi, w_lo], axis=1), b


def kernel(x, c, w_ada, b_ada, norm1_g, w_in, lambda_q1, lambda_k1, lambda_q2, lambda_k2, subln_g, w_attn_proj, sgu_ln_g, sgu_ln_b, sgu_w_s, sgu_b_s, w_sgu_proj, w_out, norm2_g, w_router_group, b_router_group, w_router_expert, b_router_expert, w_expert_gate_up, w_expert_down, final_g):
    batch, seq, d = x.shape
    assert w_ada.shape[0] == 1, "single-layer trunk"
    x2d = x.reshape(batch * seq, d)

    mod, lam = _ada(c, w_ada[0], b_ada[0], lambda_q1[0], lambda_k1[0], lambda_q2[0], lambda_k2[0])
    mod3 = mod.reshape(batch, 6, d)

    proj_a, proj_b = _in_proj(x2d, mod3, norm1_g[0], w_in[0].astype(BF16), seq)
    attn = _attention(proj_a, lam, subln_g[0], batch, seq)

    bs_full = jnp.broadcast_to(sgu_b_s[0][:, :, None], (N_SGU_GROUPS, SGU_CHUNK, SGU_CHUNK))
    wr, br = _router_weights(w_router_group[0], b_router_group[0], w_router_expert[0], b_router_expert[0])
    x1, h2, logits = _mix(attn, proj_a, proj_b, x2d, mod3, w_attn_proj[0].astype(BF16),
                          sgu_ln_g[0].reshape(1, d), sgu_ln_b[0].reshape(1, d),
                          sgu_w_s[0].astype(BF16), bs_full, w_sgu_proj[0].astype(BF16),
                          w_out[0].astype(BF16), norm2_g[0].reshape(1, d), wr, br, seq)
    comb, combt = _route(logits)
    out = _moe(h2, comb, combt, w_expert_gate_up[0].astype(BF16), w_expert_down[0].astype(BF16),
               x1, mod3, final_g, seq)
    return out.reshape(batch, seq, d)
```

```python
import functools
import math

import jax
import jax.numpy as jnp
import numpy as np
from jax import lax
from jax.experimental import pallas as pl
from jax.experimental.pallas import tpu as pltpu

D_MODEL = 1024
N_HEADS = 8
HEAD_DIM = 64
HEAD_WIDTH = 2 * HEAD_DIM
N_SGU_GROUPS = 8
SGU_CHUNK = 128
N_GROUPS = 4
EXPERTS_PER_GROUP = 8
N_EXPERTS = N_GROUPS * EXPERTS_PER_GROUP
D_EXPERT = 256
N_SEGMENTS = 7
EPS = 1e-6
LAMBDA_INIT = 0.8 - 0.6 * math.exp(-0.3 * 0)
ALIBI_SLOPES = np.array([2.0 ** (-8.0 * (h + 1) / N_HEADS) for h in range(N_HEADS)], dtype=np.float32)
ROUTER_LANES = 128
GROUP_ID_LANE = 64
MOE_TILE = 1024
MOE_CHUNK = 128
MOE_CLASSES = 3
MOE_ROW_ALIGN = 16
MOE_EXPERTS_PER_STEP = 4
LOG2_E = math.log2(math.e)
IN_PROJ_TILE = 1024
IN_PROJ_STEPS = 4
IN_PROJ_SUB = 256
MIX_TILE = 512
ATTN_ROWS = 256
ATTN_KEY_TILE = 1024
ATTN_VT_ROWS = HEAD_WIDTH + 16
ATTN_L_MIN = 2.0 ** -60
ATTN_L_MAX = 2.0 ** 100

F32 = jnp.float32
BF16 = jnp.bfloat16
VMEM_LIMIT_BYTES = 56 * 1024 * 1024


def _params(semantics):
    return pltpu.CompilerParams(dimension_semantics=semantics, vmem_limit_bytes=VMEM_LIMIT_BYTES)


def _dot(a, b):
    return jnp.dot(a, b, preferred_element_type=F32)


def _sigmoid(x):
    return 1.0 / (1.0 + jnp.exp(-x))


def _gelu_tanh(x):
    return 0.5 * x * (1.0 + jnp.tanh(math.sqrt(2.0 / math.pi) * (x + 0.044715 * (x * x * x))))


def _ada_kernel(c_ref, w_ref, b_ref, lq1_ref, lk1_ref, lq2_ref, lk2_ref, mod_ref, lam_ref):
    c = c_ref[...]
    act = c * _sigmoid(c)
    mod_ref[...] = jnp.dot(act, w_ref[...], preferred_element_type=F32,
                           precision=lax.Precision.HIGHEST) + b_ref[...]
    d1 = jnp.sum(lq1_ref[...] * lk1_ref[...], axis=-1, keepdims=True)
    d2 = jnp.sum(lq2_ref[...] * lk2_ref[...], axis=-1, keepdims=True)
    lam_ref[...] = jnp.exp(d1) - jnp.exp(d2) + LAMBDA_INIT


def _ada(c, w_ada, b_ada, lq1, lk1, lq2, lk2):
    batch, d = c.shape
    n = w_ada.shape[1]
    tn = 1024
    vec = pl.BlockSpec((1, HEAD_DIM), lambda j: (0, 0))
    return pl.pallas_call(
        _ada_kernel,
        grid=(n // tn,),
        in_specs=[pl.BlockSpec((batch, d), lambda j: (0, 0)),
                  pl.BlockSpec((d, tn), lambda j: (0, j)),
                  pl.BlockSpec((1, tn), lambda j: (0, j)),
                  vec, vec, vec, vec],
        out_specs=[pl.BlockSpec((batch, tn), lambda j: (0, j)),
                   pl.BlockSpec((1, 1), lambda j: (0, 0))],
        out_shape=[jax.ShapeDtypeStruct((batch, n), F32), jax.ShapeDtypeStruct((1, 1), F32)],
        compiler_params=_params(("arbitrary",)),
        name="ada",
    )(c, w_ada, b_ada.reshape(1, n), lq1.reshape(1, -1), lk1.reshape(1, -1),
      lq2.reshape(1, -1), lk2.reshape(1, -1))


def _in_proj_kernel(x_ref, mod_ref, g_ref, wa_ref, wb_ref, oa_ref, ob_ref, h_scr):
    j = pl.program_id(1)

    @pl.when(j == 0)
    def _():
        x = x_ref[...]
        y = x * lax.rsqrt(jnp.mean(x * x, axis=-1, keepdims=True) + EPS) * g_ref[...]
        h_scr[...] = (y * (1.0 + mod_ref[0, 1:2, :]) + mod_ref[0, 0:1, :]).astype(BF16)

    def project(epilogue_a, epilogue_b):
        for n in range(wa_ref.shape[1] // IN_PROJ_SUB):
            cs = slice(n * IN_PROJ_SUB, (n + 1) * IN_PROJ_SUB)
            oa_ref[0, :, cs] = epilogue_a(_dot(h_scr[...], wa_ref[:, cs])).astype(BF16)
            if epilogue_b is not None:
                ob_ref[0, :, cs] = epilogue_b(_dot(h_scr[...], wb_ref[:, cs])).astype(BF16)

    @pl.when(j == 0)
    def _():
        project(lambda acc: acc * (HEAD_DIM ** -0.5 * LOG2_E), _gelu_tanh)

    @pl.when(j == 1)
    def _():
        project(lambda acc: acc, _gelu_tanh)

    @pl.when(j == 2)
    def _():
        project(lambda acc: acc, _sigmoid)

    @pl.when(j == 3)
    def _():
        project(_sigmoid, None)


def _in_proj(x2d, mod3, norm_g, w_in_bf, seq):
    t, d = x2d.shape
    tm = IN_PROJ_TILE
    tiles_per_batch = seq // tm
    n_b = N_SEGMENTS - IN_PROJ_STEPS
    seg_a = lambda j: jnp.where(j < n_b, j, N_SEGMENTS - 1)
    seg_b = lambda j: n_b + jnp.minimum(j, n_b - 1)
    return pl.pallas_call(
        _in_proj_kernel,
        grid=(t // tm, IN_PROJ_STEPS),
        in_specs=[pl.BlockSpec((tm, d), lambda i, j: (i, 0)),
                  pl.BlockSpec((1, 6, d), lambda i, j: (i // tiles_per_batch, 0, 0)),
                  pl.BlockSpec((1, d), lambda i, j: (0, 0)),
                  pl.BlockSpec((d, d), lambda i, j: (0, seg_a(j))),
                  pl.BlockSpec((d, d), lambda i, j: (0, seg_b(j)))],
        out_specs=[pl.BlockSpec((1, tm, d), lambda i, j: (j, i, 0)),
                   pl.BlockSpec((1, tm, d), lambda i, j: (jnp.minimum(j, n_b - 1), i, 0))],
        out_shape=[jax.ShapeDtypeStruct((IN_PROJ_STEPS, t, d), BF16), jax.ShapeDtypeStruct((n_b, t, d), BF16)],
        scratch_shapes=[pltpu.VMEM((tm, d), BF16)],
        compiler_params=_params(("arbitrary", "arbitrary")),
        name="in_proj",
    )(x2d, mod3, norm_g.reshape(1, d), w_in_bf, w_in_bf)


def _attn_kernel(slopes_ref, lam_ref, q_ref, k_ref, v_ref, subg_ref, subg_col_ref, o_ref,
                 bias_scr, p_scr, vt_scr, *, rows, seq):
    h = pl.program_id(0)
    b = pl.program_id(1)
    nq = seq // rows
    contract_last = (((1,), (1,)), ((), ()))

    @pl.when(b == 0)
    def _():
        slope = -LOG2_E * slopes_ref[h]
        for qb in range(nq):
            kpos = lax.broadcasted_iota(jnp.int32, (seq, rows), 0)
            qpos = qb * rows + lax.broadcasted_iota(jnp.int32, (seq, rows), 1)
            bias_scr[qb] = slope * jnp.abs(qpos - kpos).astype(F32)
        sub = lax.broadcasted_iota(jnp.int32, (ATTN_VT_ROWS - HEAD_WIDTH, seq), 0)
        vt_scr[HEAD_WIDTH:, :] = jnp.where(sub == 0, 1.0, 0.0).astype(BF16)

    vt_scr[:HEAD_WIDTH, :] = v_ref[0].astype(F32).T.astype(BF16)
    lam = lam_ref[0, 0]

    def masked_q(r0):
        q = q_ref[0, pl.ds(r0, rows), :]
        lane = lax.broadcasted_iota(jnp.int32, q.shape, 1)
        zero = jnp.zeros_like(q)
        return jnp.where(lane < HEAD_DIM, q, zero), jnp.where(lane >= HEAD_DIM, q, zero)

    def fast_block(qb, n_bad):
        r0 = pl.multiple_of(qb * rows, rows)
        q0, q1 = masked_q(r0)
        qq = jnp.concatenate([q0, q1], axis=0)
        for c in range(seq // ATTN_KEY_TILE):
            ks = slice(c * ATTN_KEY_TILE, (c + 1) * ATTN_KEY_TILE)
            st = lax.dot_general(k_ref[0, ks, :], qq, contract_last, preferred_element_type=F32)
            bias = bias_scr[qb, ks, :]
            p_scr[ks, :] = jnp.exp2(st + jnp.concatenate([bias, bias], axis=1)).astype(BF16)
        oo = _dot(vt_scr[...], p_scr[...])
        l0 = oo[HEAD_WIDTH:HEAD_WIDTH + 1, :rows]
        l1 = oo[HEAD_WIDTH:HEAD_WIDTH + 1, rows:]
        ot = oo[:HEAD_WIDTH, :rows] / l0 - lam * (oo[:HEAD_WIDTH, rows:] / l1)
        ot = ot * lax.rsqrt(jnp.mean(ot * ot, axis=0, keepdims=True) + EPS)
        ot = ot * subg_col_ref[...] * (1.0 - LAMBDA_INIT)
        o_ref[pl.ds(r0, rows), :] = ot.T.astype(BF16)
        ok = ((l0 >= ATTN_L_MIN) & (l0 <= ATTN_L_MAX)) & ((l1 >= ATTN_L_MIN) & (l1 <= ATTN_L_MAX))
        return n_bad + jnp.where(ok, 0.0, 1.0)

    n_bad = lax.fori_loop(0, nq, fast_block, jnp.zeros((1, rows), F32), unroll=True)

    @pl.when(jnp.sum(n_bad) > 0.0)
    def _():
        subg = subg_ref[...]

        def safe_block(qb, carry):
            r0 = pl.multiple_of(qb * rows, rows)
            outs = []
            for qm in masked_q(r0):
                s = lax.dot_general(qm, k_ref[0], contract_last, preferred_element_type=F32) + bias_scr[qb].T
                p = jnp.exp2(s - jnp.max(s, axis=-1, keepdims=True))
                outs.append(_dot(p.astype(BF16), v_ref[0]) / jnp.sum(p, axis=-1, keepdims=True))
            o = outs[0] - lam * outs[1]
            o = o * lax.rsqrt(jnp.mean(o * o, axis=-1, keepdims=True) + EPS)
            o_ref[pl.ds(r0, rows), :] = (o * subg * (1.0 - LAMBDA_INIT)).astype(BF16)
            return carry

        lax.fori_loop(0, nq, safe_block, 0)


def _attention(qkv, lam, subln_g, batch, seq):
    t = batch * seq
    rows = ATTN_ROWS
    kernel = functools.partial(_attn_kernel, rows=rows, seq=seq)
    smem = pl.BlockSpec(memory_space=pltpu.SMEM)
    seg = lambda k: pl.BlockSpec((1, seq, HEAD_WIDTH), lambda h, b, k=k: (k, b, h))
    return pl.pallas_call(
        kernel,
        grid=(N_HEADS, batch),
        in_specs=[smem, smem, seg(0), seg(1), seg(2),
                  pl.BlockSpec((1, HEAD_WIDTH), lambda h, b: (0, 0)),
                  pl.BlockSpec((HEAD_WIDTH, 1), lambda h, b: (0, 0))],
        out_specs=pl.BlockSpec((seq, HEAD_WIDTH), lambda h, b: (b, h)),
        out_shape=jax.ShapeDtypeStruct((t, N_HEADS * HEAD_WIDTH), BF16),
        scratch_shapes=[pltpu.VMEM((seq // rows, seq, rows), F32),
                        pltpu.VMEM((seq, 2 * rows), BF16),
                        pltpu.VMEM((ATTN_VT_ROWS, seq), BF16)],
        compiler_params=_params(("arbitrary", "arbitrary")),
        name="diff_attn",
    )(jnp.asarray(ALIBI_SLOPES), lam, qkv, qkv, qkv, subln_g.reshape(1, HEAD_WIDTH),
      subln_g.reshape(HEAD_WIDTH, 1))


def _mix_kernel(attn_ref, u_ref, s_ref, ga_ref, gb_ref, x_ref, mod_ref,
                wap_ref, lng_ref, lnb_ref, ws_ref, bs_ref, wsp_ref, wout_ref,
                n2g_ref, wr_ref, br_ref,
                x1_ref, h2_ref, logit_ref, gated_scr):
    tm = x_ref.shape[0]
    y_attn = _dot(attn_ref[...], wap_ref[...])

    s = s_ref[0].astype(F32)
    mu = jnp.mean(s, axis=-1, keepdims=True)
    sc = s - mu
    var = jnp.mean(sc * sc, axis=-1, keepdims=True)
    v = ((sc * lax.rsqrt(var + EPS)) * lng_ref[...] + lnb_ref[...]).astype(BF16)
    n_chunks = tm // SGU_CHUNK
    for g in range(N_SGU_GROUPS):
        cols = slice(g * SGU_CHUNK, (g + 1) * SGU_CHUNK)
        v_g = jnp.concatenate([v[c * SGU_CHUNK:(c + 1) * SGU_CHUNK, cols] for c in range(n_chunks)], axis=1)
        mixed_g = _dot(ws_ref[g], v_g)
        for c in range(n_chunks):
            rows = slice(c * SGU_CHUNK, (c + 1) * SGU_CHUNK)
            mixed = mixed_g[:, c * SGU_CHUNK:(c + 1) * SGU_CHUNK] + bs_ref[g]
            gated_scr[rows, cols] = (u_ref[0, rows, cols].astype(F32) * mixed).astype(BF16)
    y_sgu = _dot(gated_scr[...], wsp_ref[...])

    y = ga_ref[0].astype(F32) * y_attn + gb_ref[0].astype(F32) * y_sgu
    x1 = x_ref[...] + mod_ref[0, 2:3, :] * _dot(y.astype(BF16), wout_ref[...])
    x1_ref[...] = x1

    h2 = x1 * lax.rsqrt(jnp.mean(x1 * x1, axis=-1, keepdims=True) + EPS) * n2g_ref[...]
    h2 = h2 * (1.0 + mod_ref[0, 4:5, :]) + mod_ref[0, 3:4, :]
    h2_hi = h2.astype(BF16)
    h2_ref[...] = h2_hi
    h2_lo = (h2 - h2_hi.astype(F32)).astype(BF16)
    parts = _dot(jnp.concatenate([h2_hi, h2_lo], axis=0), wr_ref[...])
    logit_ref[...] = ((parts[:tm, :ROUTER_LANES] + parts[:tm, ROUTER_LANES:])
                      + (parts[tm:, :ROUTER_LANES] + parts[tm:, ROUTER_LANES:]) + br_ref[...])


def _mix(attn, proj_a, proj_b, x2d, mod3, wap, lng, lnb, ws, bs_full, wsp, wout, n2g, wr, br, seq):
    t, d = x2d.shape
    tm = MIX_TILE
    tiles_per_batch = seq // tm
    const2 = lambda shape: pl.BlockSpec(shape, lambda i: (0, 0))
    const3 = lambda shape: pl.BlockSpec(shape, lambda i: (0, 0, 0))
    seg = lambda k: pl.BlockSpec((1, tm, d), lambda i, k=k: (k, i, 0))
    row = pl.BlockSpec((tm, d), lambda i: (i, 0))
    return pl.pallas_call(
        _mix_kernel,
        grid=(t // tm,),
        in_specs=[row, seg(0), seg(1), seg(2), seg(IN_PROJ_STEPS - 1), row,
                  pl.BlockSpec((1, 6, d), lambda i: (i // tiles_per_batch, 0, 0)),
                  const2((d, d)), const2((1, d)), const2((1, d)),
                  const3((N_SGU_GROUPS, SGU_CHUNK, SGU_CHUNK)),
                  const3((N_SGU_GROUPS, SGU_CHUNK, SGU_CHUNK)),
                  const2((d, d)), const2((d, d)), const2((1, d)),
                  const2((d, 2 * ROUTER_LANES)), const2((1, ROUTER_LANES))],
        out_specs=[row, row, pl.BlockSpec((tm, ROUTER_LANES), lambda i: (i, 0))],
        out_shape=[jax.ShapeDtypeStruct((t, d), F32), jax.ShapeDtypeStruct((t, d), BF16),
                   jax.ShapeDtypeStruct((t, ROUTER_LANES), F32)],
        scratch_shapes=[pltpu.VMEM((tm, d), BF16)],
        compiler_params=_params(("arbitrary",)),
        name="mix",
    )(attn, proj_b, proj_b, proj_b, proj_a, x2d, mod3, wap, lng, lnb, ws, bs_full, wsp, wout, n2g, wr, br)


def _route_kernel(logit_ref, comb_ref, combt_ref):
    z = logit_ref[...]
    lane = lax.broadcasted_iota(jnp.int32, z.shape, 1)
    neg = jnp.float32(-jnp.inf)
    big = jnp.int32(ROUTER_LANES)

    def first_argmax(val, vmax):
        return jnp.min(jnp.where(val == vmax, lane, big), axis=-1, keepdims=True)

    is_group = (lane >= N_EXPERTS) & (lane < N_EXPERTS + N_GROUPS)
    gl = jnp.where(is_group, z, neg)
    gmax = jnp.max(gl, axis=-1, keepdims=True)
    ge = jnp.exp(gl - gmax)
    gp = ge / jnp.sum(ge, axis=-1, keepdims=True)
    gval = jnp.max(gp, axis=-1, keepdims=True)
    gidx = first_argmax(gp, gval) - N_EXPERTS

    in_group = (lane >= gidx * EXPERTS_PER_GROUP) & (lane < (gidx + 1) * EXPERTS_PER_GROUP)
    el = jnp.where(in_group, z, neg)
    emax = jnp.max(el, axis=-1, keepdims=True)
    ee = jnp.exp(el - emax)
    ep = ee / jnp.sum(ee, axis=-1, keepdims=True)
    ep = jnp.where(in_group, ep, -1.0)
    ev0 = jnp.max(ep, axis=-1, keepdims=True)
    ei0 = first_argmax(ep, ev0)
    ep_rest = jnp.where(lane == ei0, -1.0, ep)
    ev1 = jnp.max(ep_rest, axis=-1, keepdims=True)
    ei1 = first_argmax(ep_rest, ev1)
    denom = ev0 + ev1
    comb = jnp.where(lane == ei0, ev0 / denom * gval,
                     jnp.where(lane == ei1, ev1 / denom * gval,
                               jnp.where(lane == GROUP_ID_LANE, gidx.astype(F32), 0.0)))
    comb_ref[...] = comb
    combt_ref[...] = comb.T


def _route(logits):
    t = logits.shape[0]
    tm = 1024
    spec = pl.BlockSpec((tm, ROUTER_LANES), lambda i: (i, 0))
    return pl.pallas_call(
        _route_kernel, grid=(t // tm,), in_specs=[spec],
        out_specs=[spec, pl.BlockSpec((ROUTER_LANES, tm), lambda i: (0, i))],
        out_shape=[jax.ShapeDtypeStruct((t, ROUTER_LANES), F32), jax.ShapeDtypeStruct((ROUTER_LANES, t), F32)],
        compiler_params=_params(("arbitrary",)), name="route",
    )(logits)


def _moe_kernel(h2_ref, comb_ref, combt_ref, wgu_ref, wd_ref, x1_ref, mod_ref, fg_ref, out_ref,
                onehot_scr, xs_scr, wl_scr, outc_scr, meta_ref):
    j = pl.program_id(1)
    tt = h2_ref.shape[0]
    rc = onehot_scr.shape[0]

    @pl.when(j == 0)
    def _():
        ct = combt_ref[...]
        gid = ct[GROUP_ID_LANE:GROUP_ID_LANE + 1, :]
        erow = lax.broadcasted_iota(jnp.int32, ct.shape, 0)
        first = (erow < N_EXPERTS) & ((erow & (EXPERTS_PER_GROUP - 1)) < MOE_EXPERTS_PER_STEP)
        second = (erow < N_EXPERTS) & ((erow & (EXPERTS_PER_GROUP - 1)) >= MOE_EXPERTS_PER_STEP)
        need_first = jnp.max(jnp.where(first, ct, 0.0), axis=0, keepdims=True) > 0.0
        need_second = jnp.max(jnp.where(second, ct, 0.0), axis=0, keepdims=True) > 0.0
        cls = jnp.where(need_first, jnp.where(need_second, 1.0, 0.0), 2.0)
        key = gid * MOE_CLASSES + cls
        sub = lax.broadcasted_iota(jnp.int32, (N_GROUPS * MOE_CLASSES + 4, tt), 0).astype(F32)
        member = jnp.where(key == sub, 1.0, 0.0)
        before = jnp.where(lax.broadcasted_iota(jnp.int32, (tt, tt), 0)
                           < lax.broadcasted_iota(jnp.int32, (tt, tt), 1), 1.0, 0.0).astype(BF16)
        rank = _dot(member.astype(BF16), before)
        pos = jnp.zeros((1, tt), F32)
        off = jnp.int32(0)
        n_steps = pl.num_programs(1)
        windows = lambda n_rows: (n_rows + (MOE_CHUNK - 1)) // MOE_CHUNK
        for gg in range(N_GROUPS):
            start = off
            counts = []
            for c in range(MOE_CLASSES):
                r = gg * MOE_CLASSES + c
                counts.append(jnp.sum(member[r:r + 1, :]).astype(jnp.int32))
                pos = pos + member[r:r + 1, :] * (start.astype(F32) + rank[r:r + 1, :])
                start = start + counts[c]
            second_start = (off + counts[0]) // MOE_ROW_ALIGN * MOE_ROW_ALIGN
            meta_ref[2 * gg] = off
            meta_ref[n_steps + 2 * gg] = windows(counts[0] + counts[1])
            meta_ref[2 * gg + 1] = second_start
            meta_ref[n_steps + 2 * gg + 1] = windows(start - second_start)
            off = (start + (MOE_ROW_ALIGN - 1)) // MOE_ROW_ALIGN * MOE_ROW_ALIGN
        row = lax.broadcasted_iota(jnp.int32, (rc, tt), 0).astype(F32)
        onehot = jnp.where(row == pos, 1.0, 0.0).astype(BF16)
        onehot_scr[...] = onehot
        xs_scr[:rc, :] = _dot(onehot, h2_ref[...]).astype(BF16)
        xs_scr[rc:, :] = jnp.zeros((xs_scr.shape[0] - rc, xs_scr.shape[1]), BF16)
        comb = comb_ref[...]
        c_hi = comb.astype(BF16)
        c_lo = (comb - c_hi.astype(F32)).astype(BF16)
        wl2 = _dot(onehot, jnp.concatenate([c_hi, c_lo], axis=1))
        wl_scr[:rc, :] = wl2[:, :ROUTER_LANES] + wl2[:, ROUTER_LANES:]
        wl_scr[rc:, :] = jnp.zeros((wl_scr.shape[0] - rc, ROUTER_LANES), F32)
        outc_scr[...] = jnp.zeros_like(outc_scr)

    off_g = meta_ref[j]
    n_chunks_g = meta_ref[pl.num_programs(1) + j]

    def experts(r0, rows):
        xs = xs_scr[pl.ds(r0, rows), :]
        wl = wl_scr[pl.ds(r0, rows), :]
        lane = lax.broadcasted_iota(jnp.int32, wl.shape, 1)
        acts = []
        for e in range(MOE_EXPERTS_PER_STEP):
            gu = _dot(xs, wgu_ref[e])
            gate = gu[:, :D_EXPERT]
            up = gu[:, D_EXPERT:]
            w = jnp.sum(jnp.where(lane == j * MOE_EXPERTS_PER_STEP + e, wl, 0.0), axis=-1, keepdims=True)
            acts.append((gate * _sigmoid(gate) * up * w).astype(BF16))
        wd_all = wd_ref[...].reshape(MOE_EXPERTS_PER_STEP * D_EXPERT, D_MODEL)
        outc_scr[pl.ds(r0, rows), :] += _dot(jnp.concatenate(acts, axis=1), wd_all)

    def chunk_pair(k, carry):
        experts(pl.multiple_of(off_g + k * (2 * MOE_CHUNK), MOE_ROW_ALIGN), 2 * MOE_CHUNK)
        return carry

    lax.fori_loop(0, n_chunks_g // 2, chunk_pair, 0)

    @pl.when(n_chunks_g % 2 == 1)
    def _():
        experts(pl.multiple_of(off_g + (n_chunks_g - 1) * MOE_CHUNK, MOE_ROW_ALIGN), MOE_CHUNK)

    @pl.when(j == pl.num_programs(1) - 1)
    def _():
        y = lax.dot_general(onehot_scr[...], outc_scr[:rc, :].astype(BF16), (((0,), (0,)), ((), ())),
                            preferred_element_type=F32)
        x2 = x1_ref[...] + mod_ref[0, 5:6, :] * y
        out_ref[...] = x2 * lax.rsqrt(jnp.mean(x2 * x2, axis=-1, keepdims=True) + EPS) * fg_ref[...]


def _moe(h2, comb, combt, wgu, wd, x1, mod3, final_g, seq):
    t, d = x1.shape
    tm = MOE_TILE
    tiles_per_batch = seq // tm
    n_steps = N_EXPERTS // MOE_EXPERTS_PER_STEP
    assert EXPERTS_PER_GROUP == 2 * MOE_EXPERTS_PER_STEP, "row classes assume two expert steps per group"
    compact_rows = tm + N_GROUPS * MOE_ROW_ALIGN
    buffer_rows = compact_rows + MOE_CHUNK
    row = pl.BlockSpec((tm, d), lambda i, j: (i, 0))
    return pl.pallas_call(
        _moe_kernel,
        grid=(t // tm, n_steps),
        in_specs=[row,
                  pl.BlockSpec((tm, ROUTER_LANES), lambda i, j: (i, 0)),
                  pl.BlockSpec((ROUTER_LANES, tm), lambda i, j: (0, i)),
                  pl.BlockSpec((MOE_EXPERTS_PER_STEP, d, 2 * D_EXPERT), lambda i, j: (j, 0, 0)),
                  pl.BlockSpec((MOE_EXPERTS_PER_STEP, D_EXPERT, d), lambda i, j: (j, 0, 0)),
                  row,
                  pl.BlockSpec((1, 6, d), lambda i, j: (i // tiles_per_batch, 0, 0)),
                  pl.BlockSpec((1, d), lambda i, j: (0, 0))],
        out_specs=row,
        out_shape=jax.ShapeDtypeStruct((t, d), F32),
        scratch_shapes=[pltpu.VMEM((compact_rows, tm), BF16),
                        pltpu.VMEM((buffer_rows, d), BF16),
                        pltpu.VMEM((buffer_rows, ROUTER_LANES), F32),
                        pltpu.VMEM((buffer_rows, d), F32),
                        pltpu.SMEM((2 * n_steps,), jnp.int32)],
        compiler_params=_params(("arbitrary", "arbitrary")),
        name="moe",
    )(h2, comb, combt, wgu, wd, x1, mod3, final_g.reshape(1, d))


def _router_weights(w_rg, b_rg, w_re, b_re):
    d = w_rg.shape[0]
    pad = ROUTER_LANES - N_EXPERTS - N_GROUPS
    w = jnp.concatenate([w_re, w_rg, jnp.zeros((d, pad), F32)], axis=1)
    b = jnp.concatenate([b_re, b_rg, jnp.zeros((pad,), F32)]).reshape(1, ROUTER_LANES)
    w_hi = w.astype(BF16)
    w_lo = (w - w_hi.astype(F32)).astype(BF16)
    return jnp.concatenate([w_hi, w_lo], axis=1), b


def kernel(x, c, w_ada, b_ada, norm1_g, w_in, lambda_q1, lambda_k1, lambda_q2, lambda_k2, subln_g, w_attn_proj, sgu_ln_g, sgu_ln_b, sgu_w_s, sgu_b_s, w_sgu_proj, w_out, norm2_g, w_router_group, b_router_group, w_router_expert, b_router_expert, w_expert_gate_up, w_expert_down, final_g):
    batch, seq, d = x.shape
    assert w_ada.shape[0] == 1, "single-layer trunk"
    x2d = x.reshape(batch * seq, d)

    mod, lam = _ada(c, w_ada[0], b_ada[0], lambda_q1[0], lambda_k1[0], lambda_q2[0], lambda_k2[0])
    mod3 = mod.reshape(batch, 6, d)

    proj_a, proj_b = _in_proj(x2d, mod3, norm1_g[0], w_in[0].astype(BF16), seq)
    attn = _attention(proj_a, lam, subln_g[0], batch, seq)

    bs_full = jnp.broadcast_to(sgu_b_s[0][:, :, None], (N_SGU_GROUPS, SGU_CHUNK, SGU_CHUNK))
    wr, br = _router_weights(w_router_group[0], b_router_group[0], w_router_expert[0], b_router_expert[0])
    x1, h2, logits = _mix(attn, proj_a, proj_b, x2d, mod3, w_attn_proj[0].astype(BF16),
                          sgu_ln_g[0].reshape(1, d), sgu_ln_b[0].reshape(1, d),
                          sgu_w_s[0].astype(BF16), bs_full, w_sgu_proj[0].astype(BF16),
                          w_out[0].astype(BF16), norm2_g[0].reshape(1, d), wr, br, seq)
    comb, combt = _route(logits)
    out = _moe(h2, comb, combt, w_expert_gate_up[0].astype(BF16), w_expert_down[0].astype(BF16),
               x1, mod3, final_g, seq)
    return out.reshape(batch, seq, d)
```

```python
import functools
import math

import jax
import jax.numpy as jnp
import numpy as np
from jax import lax
from jax.experimental import pallas as pl
from jax.experimental.pallas import tpu as pltpu

D_MODEL = 1024
N_HEADS = 8
HEAD_DIM = 64
HEAD_WIDTH = 2 * HEAD_DIM
N_SGU_GROUPS = 8
SGU_CHUNK = 128
N_GROUPS = 4
EXPERTS_PER_GROUP = 8
N_EXPERTS = N_GROUPS * EXPERTS_PER_GROUP
D_EXPERT = 256
N_SEGMENTS = 7
EPS = 1e-6
LAMBDA_INIT = 0.8 - 0.6 * math.exp(-0.3 * 0)
ALIBI_SLOPES = np.array([2.0 ** (-8.0 * (h + 1) / N_HEADS) for h in range(N_HEADS)], dtype=np.float32)
ROUTER_LANES = 128
ROUTER_ROWS = 40
GROUP_ID_LANE = 64
MOE_TILE = 1024
MOE_CHUNK = 128
MOE_CLASSES = 3
MOE_ROW_ALIGN = 16
MOE_EXPERTS_PER_STEP = 4
LOG2_E = math.log2(math.e)
IN_PROJ_TILE = 1024
IN_PROJ_STEPS = 4
IN_PROJ_SUB = 256
MIX_TILE = 512
ATTN_ROWS = 256
ATTN_KEY_TILE = 1024
ATTN_VT_ROWS = HEAD_WIDTH + 16
ATTN_L_MIN = 2.0 ** -60
ATTN_L_MAX = 2.0 ** 100

F32 = jnp.float32
BF16 = jnp.bfloat16
VMEM_LIMIT_BYTES = 56 * 1024 * 1024


def _params(semantics):
    return pltpu.CompilerParams(dimension_semantics=semantics, vmem_limit_bytes=VMEM_LIMIT_BYTES)


def _dot(a, b):
    return jnp.dot(a, b, preferred_element_type=F32)


def _sigmoid(x):
    return 1.0 / (1.0 + jnp.exp(-x))


def _gelu_tanh(x):
    return 0.5 * x * (1.0 + jnp.tanh(math.sqrt(2.0 / math.pi) * (x + 0.044715 * (x * x * x))))


def _ada_kernel(c_ref, w_ref, b_ref, lq1_ref, lk1_ref, lq2_ref, lk2_ref, mod_ref, lam_ref):
    c = c_ref[...]
    act = c * _sigmoid(c)
    mod_ref[...] = jnp.dot(act, w_ref[...], preferred_element_type=F32,
                           precision=lax.Precision.HIGHEST) + b_ref[...]
    d1 = jnp.sum(lq1_ref[...] * lk1_ref[...], axis=-1, keepdims=True)
    d2 = jnp.sum(lq2_ref[...] * lk2_ref[...], axis=-1, keepdims=True)
    lam_ref[...] = jnp.exp(d1) - jnp.exp(d2) + LAMBDA_INIT


def _ada(c, w_ada, b_ada, lq1, lk1, lq2, lk2):
    batch, d = c.shape
    n = w_ada.shape[1]
    tn = 1024
    vec = pl.BlockSpec((1, HEAD_DIM), lambda j: (0, 0))
    return pl.pallas_call(
        _ada_kernel,
        grid=(n // tn,),
        in_specs=[pl.BlockSpec((batch, d), lambda j: (0, 0)),
                  pl.BlockSpec((d, tn), lambda j: (0, j)),
                  pl.BlockSpec((1, tn), lambda j: (0, j)),
                  vec, vec, vec, vec],
        out_specs=[pl.BlockSpec((batch, tn), lambda j: (0, j)),
                   pl.BlockSpec((1, 1), lambda j: (0, 0))],
        out_shape=[jax.ShapeDtypeStruct((batch, n), F32), jax.ShapeDtypeStruct((1, 1), F32)],
        compiler_params=_params(("arbitrary",)),
        name="ada",
    )(c, w_ada, b_ada.reshape(1, n), lq1.reshape(1, -1), lk1.reshape(1, -1),
      lq2.reshape(1, -1), lk2.reshape(1, -1))


def _in_proj_kernel(x_ref, mod_ref, g_ref, wa_ref, wb_ref, oa_ref, ob_ref, h_scr):
    j = pl.program_id(1)

    @pl.when(j == 0)
    def _():
        x = x_ref[...]
        y = x * lax.rsqrt(jnp.mean(x * x, axis=-1, keepdims=True) + EPS) * g_ref[...]
        h_scr[...] = (y * (1.0 + mod_ref[0, 1:2, :]) + mod_ref[0, 0:1, :]).astype(BF16)

    def project(epilogue_a, epilogue_b):
        for n in range(wa_ref.shape[1] // IN_PROJ_SUB):
            cs = slice(n * IN_PROJ_SUB, (n + 1) * IN_PROJ_SUB)
            oa_ref[0, :, cs] = epilogue_a(_dot(h_scr[...], wa_ref[:, cs])).astype(BF16)
            if epilogue_b is not None:
                ob_ref[0, :, cs] = epilogue_b(_dot(h_scr[...], wb_ref[:, cs])).astype(BF16)

    @pl.when(j == 0)
    def _():
        project(lambda acc: acc * (HEAD_DIM ** -0.5 * LOG2_E), _gelu_tanh)

    @pl.when(j == 1)
    def _():
        project(lambda acc: acc, _gelu_tanh)

    @pl.when(j == 2)
    def _():
        project(lambda acc: acc, _sigmoid)

    @pl.when(j == 3)
    def _():
        project(_sigmoid, None)


def _in_proj(x2d, mod3, norm_g, w_in_bf, seq):
    t, d = x2d.shape
    tm = IN_PROJ_TILE
    tiles_per_batch = seq // tm
    n_b = N_SEGMENTS - IN_PROJ_STEPS
    seg_a = lambda j: jnp.where(j < n_b, j, N_SEGMENTS - 1)
    seg_b = lambda j: n_b + jnp.minimum(j, n_b - 1)
    return pl.pallas_call(
        _in_proj_kernel,
        grid=(t // tm, IN_PROJ_STEPS),
        in_specs=[pl.BlockSpec((tm, d), lambda i, j: (i, 0)),
                  pl.BlockSpec((1, 6, d), lambda i, j: (i // tiles_per_batch, 0, 0)),
                  pl.BlockSpec((1, d), lambda i, j: (0, 0)),
                  pl.BlockSpec((d, d), lambda i, j: (0, seg_a(j))),
                  pl.BlockSpec((d, d), lambda i, j: (0, seg_b(j)))],
        out_specs=[pl.BlockSpec((1, tm, d), lambda i, j: (j, i, 0)),
                   pl.BlockSpec((1, tm, d), lambda i, j: (jnp.minimum(j, n_b - 1), i, 0))],
        out_shape=[jax.ShapeDtypeStruct((IN_PROJ_STEPS, t, d), BF16), jax.ShapeDtypeStruct((n_b, t, d), BF16)],
        scratch_shapes=[pltpu.VMEM((tm, d), BF16)],
        compiler_params=_params(("arbitrary", "arbitrary")),
        name="in_proj",
    )(x2d, mod3, norm_g.reshape(1, d), w_in_bf, w_in_bf)


def _attn_kernel(slopes_ref, lam_ref, q_ref, k_ref, v_ref, subg_ref, subg_col_ref, o_ref,
                 bias_scr, p_scr, vt_scr, *, rows, seq):
    h = pl.program_id(0)
    b = pl.program_id(1)
    nq = seq // rows
    contract_last = (((1,), (1,)), ((), ()))

    @pl.when(b == 0)
    def _():
        slope = -LOG2_E * slopes_ref[h]
        for qb in range(nq):
            kpos = lax.broadcasted_iota(jnp.int32, (seq, rows), 0)
            qpos = qb * rows + lax.broadcasted_iota(jnp.int32, (seq, rows), 1)
            bias_scr[qb] = slope * jnp.abs(qpos - kpos).astype(F32)
        sub = lax.broadcasted_iota(jnp.int32, (ATTN_VT_ROWS - HEAD_WIDTH, seq), 0)
        vt_scr[HEAD_WIDTH:, :] = jnp.where(sub == 0, 1.0, 0.0).astype(BF16)

    vt_scr[:HEAD_WIDTH, :] = v_ref[0].astype(F32).T.astype(BF16)
    lam = lam_ref[0, 0]

    def masked_q(r0):
        q = q_ref[0, pl.ds(r0, rows), :]
        lane = lax.broadcasted_iota(jnp.int32, q.shape, 1)
        zero = jnp.zeros_like(q)
        return jnp.where(lane < HEAD_DIM, q, zero), jnp.where(lane >= HEAD_DIM, q, zero)

    def fast_block(qb, n_bad):
        r0 = pl.multiple_of(qb * rows, rows)
        q0, q1 = masked_q(r0)
        qq = jnp.concatenate([q0, q1], axis=0)
        for c in range(seq // ATTN_KEY_TILE):
            ks = slice(c * ATTN_KEY_TILE, (c + 1) * ATTN_KEY_TILE)
            st = lax.dot_general(k_ref[0, ks, :], qq, contract_last, preferred_element_type=F32)
            bias = bias_scr[qb, ks, :]
            p_scr[ks, :] = jnp.exp2(st + jnp.concatenate([bias, bias], axis=1)).astype(BF16)
        oo = _dot(vt_scr[...], p_scr[...])
        l0 = oo[HEAD_WIDTH:HEAD_WIDTH + 1, :rows]
        l1 = oo[HEAD_WIDTH:HEAD_WIDTH + 1, rows:]
        ot = oo[:HEAD_WIDTH, :rows] / l0 - lam * (oo[:HEAD_WIDTH, rows:] / l1)
        ot = ot * lax.rsqrt(jnp.mean(ot * ot, axis=0, keepdims=True) + EPS)
        ot = ot * subg_col_ref[...] * (1.0 - LAMBDA_INIT)
        o_ref[pl.ds(r0, rows), :] = ot.T.astype(BF16)
        ok = ((l0 >= ATTN_L_MIN) & (l0 <= ATTN_L_MAX)) & ((l1 >= ATTN_L_MIN) & (l1 <= ATTN_L_MAX))
        return n_bad + jnp.where(ok, 0.0, 1.0)

    n_bad = lax.fori_loop(0, nq, fast_block, jnp.zeros((1, rows), F32), unroll=True)

    @pl.when(jnp.sum(n_bad) > 0.0)
    def _():
        subg = subg_ref[...]

        def safe_block(qb, carry):
            r0 = pl.multiple_of(qb * rows, rows)
            outs = []
            for qm in masked_q(r0):
                s = lax.dot_general(qm, k_ref[0], contract_last, preferred_element_type=F32) + bias_scr[qb].T
                p = jnp.exp2(s - jnp.max(s, axis=-1, keepdims=True))
                outs.append(_dot(p.astype(BF16), v_ref[0]) / jnp.sum(p, axis=-1, keepdims=True))
            o = outs[0] - lam * outs[1]
            o = o * lax.rsqrt(jnp.mean(o * o, axis=-1, keepdims=True) + EPS)
            o_ref[pl.ds(r0, rows), :] = (o * subg * (1.0 - LAMBDA_INIT)).astype(BF16)
            return carry

        lax.fori_loop(0, nq, safe_block, 0)


def _attention(qkv, lam, subln_g, batch, seq):
    t = batch * seq
    rows = ATTN_ROWS
    kernel = functools.partial(_attn_kernel, rows=rows, seq=seq)
    smem = pl.BlockSpec(memory_space=pltpu.SMEM)
    seg = lambda k: pl.BlockSpec((1, seq, HEAD_WIDTH), lambda h, b, k=k: (k, b, h))
    return pl.pallas_call(
        kernel,
        grid=(N_HEADS, batch),
        in_specs=[smem, smem, seg(0), seg(1), seg(2),
                  pl.BlockSpec((1, HEAD_WIDTH), lambda h, b: (0, 0)),
                  pl.BlockSpec((HEAD_WIDTH, 1), lambda h, b: (0, 0))],
        out_specs=pl.BlockSpec((seq, HEAD_WIDTH), lambda h, b: (b, h)),
        out_shape=jax.ShapeDtypeStruct((t, N_HEADS * HEAD_WIDTH), BF16),
        scratch_shapes=[pltpu.VMEM((seq // rows, seq, rows), F32),
                        pltpu.VMEM((seq, 2 * rows), BF16),
                        pltpu.VMEM((ATTN_VT_ROWS, seq), BF16)],
        compiler_params=_params(("arbitrary", "arbitrary")),
        name="diff_attn",
    )(jnp.asarray(ALIBI_SLOPES), lam, qkv, qkv, qkv, subln_g.reshape(1, HEAD_WIDTH),
      subln_g.reshape(HEAD_WIDTH, 1))


def _mix_kernel(attn_ref, u_ref, s_ref, ga_ref, gb_ref, x_ref, mod_ref,
                wap_ref, lng_ref, lnb_ref, ws_ref, bs_ref, wsp_ref, wout_ref,
                n2g_ref, wr_ref, br_ref,
                x1_ref, h2_ref, logit_ref, gated_scr):
    tm = x_ref.shape[0]
    y_attn = _dot(attn_ref[...], wap_ref[...])

    s = s_ref[0].astype(F32)
    mu = jnp.mean(s, axis=-1, keepdims=True)
    sc = s - mu
    var = jnp.mean(sc * sc, axis=-1, keepdims=True)
    v = ((sc * lax.rsqrt(var + EPS)) * lng_ref[...] + lnb_ref[...]).astype(BF16)
    n_chunks = tm // SGU_CHUNK
    for g in range(N_SGU_GROUPS):
        cols = slice(g * SGU_CHUNK, (g + 1) * SGU_CHUNK)
        v_g = jnp.concatenate([v[c * SGU_CHUNK:(c + 1) * SGU_CHUNK, cols] for c in range(n_chunks)], axis=1)
        mixed_g = _dot(ws_ref[g], v_g)
        for c in range(n_chunks):
            rows = slice(c * SGU_CHUNK, (c + 1) * SGU_CHUNK)
            mixed = mixed_g[:, c * SGU_CHUNK:(c + 1) * SGU_CHUNK] + bs_ref[g]
            gated_scr[rows, cols] = (u_ref[0, rows, cols].astype(F32) * mixed).astype(BF16)
    y_sgu = _dot(gated_scr[...], wsp_ref[...])

    y = ga_ref[0].astype(F32) * y_attn + gb_ref[0].astype(F32) * y_sgu
    x1 = x_ref[...] + mod_ref[0, 2:3, :] * _dot(y.astype(BF16), wout_ref[...])
    x1_ref[...] = x1

    h2 = x1 * lax.rsqrt(jnp.mean(x1 * x1, axis=-1, keepdims=True) + EPS) * n2g_ref[...]
    h2 = h2 * (1.0 + mod_ref[0, 4:5, :]) + mod_ref[0, 3:4, :]
    h2_hi = h2.astype(BF16)
    h2_ref[...] = h2_hi
    h2_lo = (h2 - h2_hi.astype(F32)).astype(BF16)
    parts = _dot(jnp.concatenate([h2_hi, h2_lo], axis=0), wr_ref[...])
    logits = ((parts[:tm, :ROUTER_LANES] + parts[:tm, ROUTER_LANES:])
              + (parts[tm:, :ROUTER_LANES] + parts[tm:, ROUTER_LANES:]) + br_ref[...])
    logit_ref[...] = logits.T


def _mix(attn, proj_a, proj_b, x2d, mod3, wap, lng, lnb, ws, bs_full, wsp, wout, n2g, wr, br, seq):
    t, d = x2d.shape
    tm = MIX_TILE
    tiles_per_batch = seq // tm
    const2 = lambda shape: pl.BlockSpec(shape, lambda i: (0, 0))
    const3 = lambda shape: pl.BlockSpec(shape, lambda i: (0, 0, 0))
    seg = lambda k: pl.BlockSpec((1, tm, d), lambda i, k=k: (k, i, 0))
    row = pl.BlockSpec((tm, d), lambda i: (i, 0))
    return pl.pallas_call(
        _mix_kernel,
        grid=(t // tm,),
        in_specs=[row, seg(0), seg(1), seg(2), seg(IN_PROJ_STEPS - 1), row,
                  pl.BlockSpec((1, 6, d), lambda i: (i // tiles_per_batch, 0, 0)),
                  const2((d, d)), const2((1, d)), const2((1, d)),
                  const3((N_SGU_GROUPS, SGU_CHUNK, SGU_CHUNK)),
                  const3((N_SGU_GROUPS, SGU_CHUNK, SGU_CHUNK)),
                  const2((d, d)), const2((d, d)), const2((1, d)),
                  const2((d, 2 * ROUTER_LANES)), const2((1, ROUTER_LANES))],
        out_specs=[row, row, pl.BlockSpec((ROUTER_LANES, tm), lambda i: (0, i))],
        out_shape=[jax.ShapeDtypeStruct((t, d), F32), jax.ShapeDtypeStruct((t, d), BF16),
                   jax.ShapeDtypeStruct((ROUTER_LANES, t), F32)],
        scratch_shapes=[pltpu.VMEM((tm, d), BF16)],
        compiler_params=_params(("arbitrary",)),
        name="mix",
    )(attn, proj_b, proj_b, proj_b, proj_a, x2d, mod3, wap, lng, lnb, ws, bs_full, wsp, wout, n2g, wr, br)


def _route_kernel(logit_ref, comb_ref, combt_ref):
    z = logit_ref[:ROUTER_ROWS, :]
    row = lax.broadcasted_iota(jnp.int32, z.shape, 0)
    neg = jnp.float32(-jnp.inf)
    big = jnp.int32(ROUTER_LANES)

    def first_argmax(val, vmax):
        return jnp.min(jnp.where(val == vmax, row, big), axis=0, keepdims=True)

    is_group = (row >= N_EXPERTS) & (row < N_EXPERTS + N_GROUPS)
    gl = jnp.where(is_group, z, neg)
    gmax = jnp.max(gl, axis=0, keepdims=True)
    ge = jnp.exp(gl - gmax)
    gp = ge / jnp.sum(ge, axis=0, keepdims=True)
    gval = jnp.max(gp, axis=0, keepdims=True)
    gidx = first_argmax(gp, gval) - N_EXPERTS

    in_group = (row >= gidx * EXPERTS_PER_GROUP) & (row < (gidx + 1) * EXPERTS_PER_GROUP)
    el = jnp.where(in_group, z, neg)
    emax = jnp.max(el, axis=0, keepdims=True)
    ee = jnp.exp(el - emax)
    ep = ee / jnp.sum(ee, axis=0, keepdims=True)
    ep = jnp.where(in_group, ep, -1.0)
    ev0 = jnp.max(ep, axis=0, keepdims=True)
    ei0 = first_argmax(ep, ev0)
    ep_rest = jnp.where(row == ei0, -1.0, ep)
    ev1 = jnp.max(ep_rest, axis=0, keepdims=True)
    ei1 = first_argmax(ep_rest, ev1)
    denom = ev0 + ev1
    weights = jnp.where(row == ei0, ev0 / denom * gval, jnp.where(row == ei1, ev1 / denom * gval, 0.0))
    tm = z.shape[1]
    combt = jnp.concatenate(
        [weights, jnp.zeros((GROUP_ID_LANE - ROUTER_ROWS, tm), F32),
         jnp.broadcast_to(gidx.astype(F32), (8, tm)) * (lax.broadcasted_iota(jnp.int32, (8, tm), 0) == 0),
         jnp.zeros((ROUTER_LANES - GROUP_ID_LANE - 8, tm), F32)], axis=0)
    combt_ref[...] = combt
    comb_ref[...] = combt.T


def _route(logits_t):
    t = logits_t.shape[1]
    tm = 1024
    spec_t = pl.BlockSpec((ROUTER_LANES, tm), lambda i: (0, i))
    return pl.pallas_call(
        _route_kernel, grid=(t // tm,), in_specs=[spec_t],
        out_specs=[pl.BlockSpec((tm, ROUTER_LANES), lambda i: (i, 0)), spec_t],
        out_shape=[jax.ShapeDtypeStruct((t, ROUTER_LANES), F32), jax.ShapeDtypeStruct((ROUTER_LANES, t), F32)],
        compiler_params=_params(("arbitrary",)), name="route",
    )(logits_t)


def _moe_kernel(h2_ref, comb_ref, combt_ref, wgu_ref, wd_ref, x1_ref, mod_ref, fg_ref, out_ref,
                onehot_scr, xs_scr, wl_scr, outc_scr, meta_ref):
    j = pl.program_id(1)
    tt = h2_ref.shape[0]
    rc = onehot_scr.shape[0]

    @pl.when(j == 0)
    def _():
        ct = combt_ref[...]
        gid = ct[GROUP_ID_LANE:GROUP_ID_LANE + 1, :]
        erow = lax.broadcasted_iota(jnp.int32, ct.shape, 0)
        first = (erow < N_EXPERTS) & ((erow & (EXPERTS_PER_GROUP - 1)) < MOE_EXPERTS_PER_STEP)
        second = (erow < N_EXPERTS) & ((erow & (EXPERTS_PER_GROUP - 1)) >= MOE_EXPERTS_PER_STEP)
        need_first = jnp.max(jnp.where(first, ct, 0.0), axis=0, keepdims=True) > 0.0
        need_second = jnp.max(jnp.where(second, ct, 0.0), axis=0, keepdims=True) > 0.0
        cls = jnp.where(need_first, jnp.where(need_second, 1.0, 0.0), 2.0)
        key = gid * MOE_CLASSES + cls
        sub = lax.broadcasted_iota(jnp.int32, (N_GROUPS * MOE_CLASSES + 4, tt), 0).astype(F32)
        member = jnp.where(key == sub, 1.0, 0.0)
        before = jnp.where(lax.broadcasted_iota(jnp.int32, (tt, tt), 0)
                           < lax.broadcasted_iota(jnp.int32, (tt, tt), 1), 1.0, 0.0).astype(BF16)
        rank = _dot(member.astype(BF16), before)
        pos = jnp.zeros((1, tt), F32)
        off = jnp.int32(0)
        n_steps = pl.num_programs(1)
        windows = lambda n_rows: (n_rows + (MOE_CHUNK - 1)) // MOE_CHUNK
        for gg in range(N_GROUPS):
            start = off
            counts = []
            for c in range(MOE_CLASSES):
                r = gg * MOE_CLASSES + c
                counts.append(jnp.sum(member[r:r + 1, :]).astype(jnp.int32))
                pos = pos + member[r:r + 1, :] * (start.astype(F32) + rank[r:r + 1, :])
                start = start + counts[c]
            second_start = (off + counts[0]) // MOE_ROW_ALIGN * MOE_ROW_ALIGN
            meta_ref[2 * gg] = off
            meta_ref[n_steps + 2 * gg] = windows(counts[0] + counts[1])
            meta_ref[2 * gg + 1] = second_start
            meta_ref[n_steps + 2 * gg + 1] = windows(start - second_start)
            off = (start + (MOE_ROW_ALIGN - 1)) // MOE_ROW_ALIGN * MOE_ROW_ALIGN
        row = lax.broadcasted_iota(jnp.int32, (rc, tt), 0).astype(F32)
        onehot = jnp.where(row == pos, 1.0, 0.0).astype(BF16)
        onehot_scr[...] = onehot
        xs_scr[:rc, :] = _dot(onehot, h2_ref[...]).astype(BF16)
        xs_scr[rc:, :] = jnp.zeros((xs_scr.shape[0] - rc, xs_scr.shape[1]), BF16)
        comb = comb_ref[...]
        c_hi = comb.astype(BF16)
        c_lo = (comb - c_hi.astype(F32)).astype(BF16)
        wl2 = _dot(onehot, jnp.concatenate([c_hi, c_lo], axis=1))
        wl_scr[:rc, :] = wl2[:, :ROUTER_LANES] + wl2[:, ROUTER_LANES:]
        wl_scr[rc:, :] = jnp.zeros((wl_scr.shape[0] - rc, ROUTER_LANES), F32)
        outc_scr[...] = jnp.zeros_like(outc_scr)

    off_g = meta_ref[j]
    n_chunks_g = meta_ref[pl.num_programs(1) + j]

    def experts(r0, rows):
        xs = xs_scr[pl.ds(r0, rows), :]
        wl = wl_scr[pl.ds(r0, rows), :]
        lane = lax.broadcasted_iota(jnp.int32, wl.shape, 1)
        acts = []
        for e in range(MOE_EXPERTS_PER_STEP):
            gu = _dot(xs, wgu_ref[e])
            gate = gu[:, :D_EXPERT]
            up = gu[:, D_EXPERT:]
            w = jnp.sum(jnp.where(lane == j * MOE_EXPERTS_PER_STEP + e, wl, 0.0), axis=-1, keepdims=True)
            acts.append((gate * _sigmoid(gate) * up * w).astype(BF16))
        wd_all = wd_ref[...].reshape(MOE_EXPERTS_PER_STEP * D_EXPERT, D_MODEL)
        outc_scr[pl.ds(r0, rows), :] += _dot(jnp.concatenate(acts, axis=1), wd_all)

    def chunk_pair(k, carry):
        experts(pl.multiple_of(off_g + k * (2 * MOE_CHUNK), MOE_ROW_ALIGN), 2 * MOE_CHUNK)
        return carry

    lax.fori_loop(0, n_chunks_g // 2, chunk_pair, 0)

    @pl.when(n_chunks_g % 2 == 1)
    def _():
        experts(pl.multiple_of(off_g + (n_chunks_g - 1) * MOE_CHUNK, MOE_ROW_ALIGN), MOE_CHUNK)

    @pl.when(j == pl.num_programs(1) - 1)
    def _():
        y = lax.dot_general(onehot_scr[...], outc_scr[:rc, :].astype(BF16), (((0,), (0,)), ((), ())),
                            preferred_element_type=F32)
        x2 = x1_ref[...] + mod_ref[0, 5:6, :] * y
        out_ref[...] = x2 * lax.rsqrt(jnp.mean(x2 * x2, axis=-1, keepdims=True) + EPS) * fg_ref[...]


def _moe(h2, comb, combt, wgu, wd, x1, mod3, final_g, seq):
    t, d = x1.shape
    tm = MOE_TILE
    tiles_per_batch = seq // tm
    n_steps = N_EXPERTS // MOE_EXPERTS_PER_STEP
    assert EXPERTS_PER_GROUP == 2 * MOE_EXPERTS_PER_STEP, "row classes assume two expert steps per group"
    compact_rows = tm + N_GROUPS * MOE_ROW_ALIGN
    buffer_rows = compact_rows + MOE_CHUNK
    row = pl.BlockSpec((tm, d), lambda i, j: (i, 0))
    return pl.pallas_call(
        _moe_kernel,
        grid=(t // tm, n_steps),
        in_specs=[row,
                  pl.BlockSpec((tm, ROUTER_LANES), lambda i, j: (i, 0)),
                  pl.BlockSpec((ROUTER_LANES, tm), lambda i, j: (0, i)),
                  pl.BlockSpec((MOE_EXPERTS_PER_STEP, d, 2 * D_EXPERT), lambda i, j: (j, 0, 0)),
                  pl.BlockSpec((MOE_EXPERTS_PER_STEP, D_EXPERT, d), lambda i, j: (j, 0, 0)),
                  row,
                  pl.BlockSpec((1, 6, d), lambda i, j: (i // tiles_per_batch, 0, 0)),
                  pl.BlockSpec((1, d), lambda i, j: (0, 0))],
        out_specs=row,
        out_shape=jax.ShapeDtypeStruct((t, d), F32),
        scratch_shapes=[pltpu.VMEM((compact_rows, tm), BF16),
                        pltpu.VMEM((buffer_rows, d), BF16),
                        pltpu.VMEM((buffer_rows, ROUTER_LANES), F32),
                        pltpu.VMEM((buffer_rows, d), F32),
                        pltpu.SMEM((2 * n_steps,), jnp.int32)],
        compiler_params=_params(("arbitrary", "arbitrary")),
        name="moe",
    )(h2, comb, combt, wgu, wd, x1, mod3, final_g.reshape(1, d))


def _router_weights(w_rg, b_rg, w_re, b_re):
    d = w_rg.shape[0]
    pad = ROUTER_LANES - N_EXPERTS - N_GROUPS
    w = jnp.concatenate([w_re, w_rg, jnp.zeros((d, pad), F32)], axis=1)
    b = jnp.concatenate([b_re, b_rg, jnp.zeros((pad,), F32)]).reshape(1, ROUTER_LANES)
    w_hi = w.astype(BF16)
    w_lo = (w - w_hi.astype(F32)).astype(BF16)
    return jnp.concatenate([w_hi, w_lo], axis=1), b


def kernel(x, c, w_ada, b_ada, norm1_g, w_in, lambda_q1, lambda_k1, lambda_q2, lambda_k2, subln_g, w_attn_proj, sgu_ln_g, sgu_ln_b, sgu_w_s, sgu_b_s, w_sgu_proj, w_out, norm2_g, w_router_group, b_router_group, w_router_expert, b_router_expert, w_expert_gate_up, w_expert_down, final_g):
    batch, seq, d = x.shape
    assert w_ada.shape[0] == 1, "single-layer trunk"
    x2d = x.reshape(batch * seq, d)

    mod, lam = _ada(c, w_ada[0], b_ada[0], lambda_q1[0], lambda_k1[0], lambda_q2[0], lambda_k2[0])
    mod3 = mod.reshape(batch, 6, d)

    proj_a, proj_b = _in_proj(x2d, mod3, norm1_g[0], w_in[0].astype(BF16), seq)
    attn = _attention(proj_a, lam, subln_g[0], batch, seq)

    bs_full = jnp.broadcast_to(sgu_b_s[0][:, :, None], (N_SGU_GROUPS, SGU_CHUNK, SGU_CHUNK))
    wr, br = _router_weights(w_router_group[0], b_router_group[0], w_router_expert[0], b_router_expert[0])
    x1, h2, logits = _mix(attn, proj_a, proj_b, x2d, mod3, w_attn_proj[0].astype(BF16),
                          sgu_ln_g[0].reshape(1, d), sgu_ln_b[0].reshape(1, d),
                          sgu_w_s[0].astype(BF16), bs_full, w_sgu_proj[0].astype(BF16),
                          w_out[0].astype(BF16), norm2_g[0].reshape(1, d), wr, br, seq)
    comb, combt = _route(logits)
    out = _moe(h2, comb, combt, w_expert_gate_up[0].astype(BF16), w_expert_down[0].astype(BF16),
               x1, mod3, final_g, seq)
    return out.reshape(batch, seq, d)
```

```python
import functools
import math

import jax
import jax.numpy as jnp
import numpy as np
from jax import lax
from jax.experimental import pallas as pl
from jax.experimental.pallas import tpu as pltpu

D_MODEL = 1024
N_HEADS = 8
HEAD_DIM = 64
HEAD_WIDTH = 2 * HEAD_DIM
N_SGU_GROUPS = 8
SGU_CHUNK = 128
N_GROUPS = 4
EXPERTS_PER_GROUP = 8
N_EXPERTS = N_GROUPS * EXPERTS_PER_GROUP
D_EXPERT = 256
N_SEGMENTS = 7
EPS = 1e-6
LAMBDA_INIT = 0.8 - 0.6 * math.exp(-0.3 * 0)
ALIBI_SLOPES = np.array([2.0 ** (-8.0 * (h + 1) / N_HEADS) for h in range(N_HEADS)], dtype=np.float32)
ROUTER_LANES = 128
ROUTE_TILE = 4096
ROUTER_ROWS = 40
GROUP_ID_LANE = 64
MOE_TILE = 1024
MOE_CHUNK = 128
MOE_CLASSES = 3
MOE_ROW_ALIGN = 16
MOE_EXPERTS_PER_STEP = 4
LOG2_E = math.log2(math.e)
IN_PROJ_TILE = 1024
IN_PROJ_STEPS = 3
IN_PROJ_SUB = 256
MIX_TILE = 512
ATTN_ROWS = 256
ATTN_KEY_TILE = 1024
ATTN_VT_ROWS = HEAD_WIDTH + 16
ATTN_L_MIN = 2.0 ** -60
ATTN_L_MAX = 2.0 ** 100

F32 = jnp.float32
BF16 = jnp.bfloat16
VMEM_LIMIT_BYTES = 56 * 1024 * 1024


def _params(semantics):
    return pltpu.CompilerParams(dimension_semantics=semantics, vmem_limit_bytes=VMEM_LIMIT_BYTES)


def _dot(a, b):
    return jnp.dot(a, b, preferred_element_type=F32)


def _sigmoid(x):
    return 1.0 / (1.0 + jnp.exp(-x))


def _gelu_tanh(x):
    return 0.5 * x * (1.0 + jnp.tanh(math.sqrt(2.0 / math.pi) * (x + 0.044715 * (x * x * x))))


def _ada_kernel(c_ref, w_ref, b_ref, lq1_ref, lk1_ref, lq2_ref, lk2_ref, mod_ref, lam_ref):
    c = c_ref[...]
    act = c * _sigmoid(c)
    mod_ref[...] = jnp.dot(act, w_ref[...], preferred_element_type=F32,
                           precision=lax.Precision.HIGHEST) + b_ref[...]
    d1 = jnp.sum(lq1_ref[...] * lk1_ref[...], axis=-1, keepdims=True)
    d2 = jnp.sum(lq2_ref[...] * lk2_ref[...], axis=-1, keepdims=True)
    lam_ref[...] = jnp.exp(d1) - jnp.exp(d2) + LAMBDA_INIT


def _ada(c, w_ada, b_ada, lq1, lk1, lq2, lk2):
    batch, d = c.shape
    n = w_ada.shape[1]
    tn = 1024
    vec = pl.BlockSpec((1, HEAD_DIM), lambda j: (0, 0))
    return pl.pallas_call(
        _ada_kernel,
        grid=(n // tn,),
        in_specs=[pl.BlockSpec((batch, d), lambda j: (0, 0)),
                  pl.BlockSpec((d, tn), lambda j: (0, j)),
                  pl.BlockSpec((1, tn), lambda j: (0, j)),
                  vec, vec, vec, vec],
        out_specs=[pl.BlockSpec((batch, tn), lambda j: (0, j)),
                   pl.BlockSpec((1, 1), lambda j: (0, 0))],
        out_shape=[jax.ShapeDtypeStruct((batch, n), F32), jax.ShapeDtypeStruct((1, 1), F32)],
        compiler_params=_params(("arbitrary",)),
        name="ada",
    )(c, w_ada, b_ada.reshape(1, n), lq1.reshape(1, -1), lk1.reshape(1, -1),
      lq2.reshape(1, -1), lk2.reshape(1, -1))


def _in_proj_kernel(x_ref, mod_ref, g_ref, wa_ref, wb_ref, wc_ref, oa_ref, ob_ref, oc_ref, h_scr):
    j = pl.program_id(1)

    @pl.when(j == 0)
    def _():
        x = x_ref[...]
        y = x * lax.rsqrt(jnp.mean(x * x, axis=-1, keepdims=True) + EPS) * g_ref[...]
        h_scr[...] = (y * (1.0 + mod_ref[0, 1:2, :]) + mod_ref[0, 0:1, :]).astype(BF16)

    def project(epilogue_a, epilogue_b, gate_subtiles):
        n_sub = wa_ref.shape[1] // IN_PROJ_SUB
        for n in range(n_sub):
            cs = slice(n * IN_PROJ_SUB, (n + 1) * IN_PROJ_SUB)
            oa_ref[0, :, cs] = epilogue_a(_dot(h_scr[...], wa_ref[:, cs])).astype(BF16)
            ob_ref[0, :, cs] = epilogue_b(_dot(h_scr[...], wb_ref[:, cs])).astype(BF16)
            if n < len(gate_subtiles):
                gs = slice(gate_subtiles[n] * IN_PROJ_SUB, (gate_subtiles[n] + 1) * IN_PROJ_SUB)
                oc_ref[:, gs] = _sigmoid(_dot(h_scr[...], wc_ref[:, gs])).astype(BF16)

    @pl.when(j == 0)
    def _():
        project(lambda acc: acc * (HEAD_DIM ** -0.5 * LOG2_E), _gelu_tanh, (0, 1))

    @pl.when(j == 1)
    def _():
        project(lambda acc: acc, _gelu_tanh, (2,))

    @pl.when(j == 2)
    def _():
        project(lambda acc: acc, _sigmoid, (3,))


def _in_proj(x2d, mod3, norm_g, w_in_bf, seq):
    t, d = x2d.shape
    tm = IN_PROJ_TILE
    tiles_per_batch = seq // tm
    assert d // IN_PROJ_SUB == 4 and N_SEGMENTS == 2 * IN_PROJ_STEPS + 1
    return pl.pallas_call(
        _in_proj_kernel,
        grid=(t // tm, IN_PROJ_STEPS),
        in_specs=[pl.BlockSpec((tm, d), lambda i, j: (i, 0)),
                  pl.BlockSpec((1, 6, d), lambda i, j: (i // tiles_per_batch, 0, 0)),
                  pl.BlockSpec((1, d), lambda i, j: (0, 0)),
                  pl.BlockSpec((d, d), lambda i, j: (0, j)),
                  pl.BlockSpec((d, d), lambda i, j: (0, IN_PROJ_STEPS + j)),
                  pl.BlockSpec((d, d), lambda i, j: (0, N_SEGMENTS - 1))],
        out_specs=[pl.BlockSpec((1, tm, d), lambda i, j: (j, i, 0)),
                   pl.BlockSpec((1, tm, d), lambda i, j: (j, i, 0)),
                   pl.BlockSpec((tm, d), lambda i, j: (i, 0))],
        out_shape=[jax.ShapeDtypeStruct((IN_PROJ_STEPS, t, d), BF16), jax.ShapeDtypeStruct((IN_PROJ_STEPS, t, d), BF16),
                   jax.ShapeDtypeStruct((t, d), BF16)],
        scratch_shapes=[pltpu.VMEM((tm, d), BF16)],
        compiler_params=_params(("arbitrary", "arbitrary")),
        name="in_proj",
    )(x2d, mod3, norm_g.reshape(1, d), w_in_bf, w_in_bf, w_in_bf)


def _attn_kernel(slopes_ref, lam_ref, q_ref, k_ref, v_ref, subg_ref, subg_col_ref, o_ref,
                 bias_scr, p_scr, vt_scr, *, rows, seq):
    h = pl.program_id(0)
    b = pl.program_id(1)
    nq = seq // rows
    contract_last = (((1,), (1,)), ((), ()))

    @pl.when(b == 0)
    def _():
        slope = -LOG2_E * slopes_ref[h]
        for qb in range(nq):
            kpos = lax.broadcasted_iota(jnp.int32, (seq, rows), 0)
            qpos = qb * rows + lax.broadcasted_iota(jnp.int32, (seq, rows), 1)
            bias_scr[qb] = slope * jnp.abs(qpos - kpos).astype(F32)
        sub = lax.broadcasted_iota(jnp.int32, (ATTN_VT_ROWS - HEAD_WIDTH, seq), 0)
        vt_scr[HEAD_WIDTH:, :] = jnp.where(sub == 0, 1.0, 0.0).astype(BF16)

    vt_scr[:HEAD_WIDTH, :] = v_ref[0].astype(F32).T.astype(BF16)
    lam = lam_ref[0, 0]

    def masked_q(r0):
        q = q_ref[0, pl.ds(r0, rows), :]
        lane = lax.broadcasted_iota(jnp.int32, q.shape, 1)
        zero = jnp.zeros_like(q)
        return jnp.where(lane < HEAD_DIM, q, zero), jnp.where(lane >= HEAD_DIM, q, zero)

    def fast_block(qb, n_bad):
        r0 = pl.multiple_of(qb * rows, rows)
        q0, q1 = masked_q(r0)
        qq = jnp.concatenate([q0, q1], axis=0)
        for c in range(seq // ATTN_KEY_TILE):
            ks = slice(c * ATTN_KEY_TILE, (c + 1) * ATTN_KEY_TILE)
            st = lax.dot_general(k_ref[0, ks, :], qq, contract_last, preferred_element_type=F32)
            bias = bias_scr[qb, ks, :]
            p_scr[ks, :] = jnp.exp2(st + jnp.concatenate([bias, bias], axis=1)).astype(BF16)
        oo = _dot(vt_scr[...], p_scr[...])
        l0 = oo[HEAD_WIDTH:HEAD_WIDTH + 1, :rows]
        l1 = oo[HEAD_WIDTH:HEAD_WIDTH + 1, rows:]
        ot = oo[:HEAD_WIDTH, :rows] / l0 - lam * (oo[:HEAD_WIDTH, rows:] / l1)
        ot = ot * lax.rsqrt(jnp.mean(ot * ot, axis=0, keepdims=True) + EPS)
        ot = ot * subg_col_ref[...] * (1.0 - LAMBDA_INIT)
        o_ref[pl.ds(r0, rows), :] = ot.T.astype(BF16)
        ok = ((l0 >= ATTN_L_MIN) & (l0 <= ATTN_L_MAX)) & ((l1 >= ATTN_L_MIN) & (l1 <= ATTN_L_MAX))
        return n_bad + jnp.where(ok, 0.0, 1.0)

    n_bad = lax.fori_loop(0, nq, fast_block, jnp.zeros((1, rows), F32), unroll=True)

    @pl.when(jnp.sum(n_bad) > 0.0)
    def _():
        subg = subg_ref[...]

        def safe_block(qb, carry):
            r0 = pl.multiple_of(qb * rows, rows)
            outs = []
            for qm in masked_q(r0):
                s = lax.dot_general(qm, k_ref[0], contract_last, preferred_element_type=F32) + bias_scr[qb].T
                p = jnp.exp2(s - jnp.max(s, axis=-1, keepdims=True))
                outs.append(_dot(p.astype(BF16), v_ref[0]) / jnp.sum(p, axis=-1, keepdims=True))
            o = outs[0] - lam * outs[1]
            o = o * lax.rsqrt(jnp.mean(o * o, axis=-1, keepdims=True) + EPS)
            o_ref[pl.ds(r0, rows), :] = (o * subg * (1.0 - LAMBDA_INIT)).astype(BF16)
            return carry

        lax.fori_loop(0, nq, safe_block, 0)


def _attention(qkv, lam, subln_g, batch, seq):
    t = batch * seq
    rows = ATTN_ROWS
    kernel = functools.partial(_attn_kernel, rows=rows, seq=seq)
    smem = pl.BlockSpec(memory_space=pltpu.SMEM)
    seg = lambda k: pl.BlockSpec((1, seq, HEAD_WIDTH), lambda h, b, k=k: (k, b, h))
    return pl.pallas_call(
        kernel,
        grid=(N_HEADS, batch),
        in_specs=[smem, smem, seg(0), seg(1), seg(2),
                  pl.BlockSpec((1, HEAD_WIDTH), lambda h, b: (0, 0)),
                  pl.BlockSpec((HEAD_WIDTH, 1), lambda h, b: (0, 0))],
        out_specs=pl.BlockSpec((seq, HEAD_WIDTH), lambda h, b: (b, h)),
        out_shape=jax.ShapeDtypeStruct((t, N_HEADS * HEAD_WIDTH), BF16),
        scratch_shapes=[pltpu.VMEM((seq // rows, seq, rows), F32),
                        pltpu.VMEM((seq, 2 * rows), BF16),
                        pltpu.VMEM((ATTN_VT_ROWS, seq), BF16)],
        compiler_params=_params(("arbitrary", "arbitrary")),
        name="diff_attn",
    )(jnp.asarray(ALIBI_SLOPES), lam, qkv, qkv, qkv, subln_g.reshape(1, HEAD_WIDTH),
      subln_g.reshape(HEAD_WIDTH, 1))


def _mix_kernel(attn_ref, u_ref, s_ref, ga_ref, gb_ref, x_ref, mod_ref,
                wap_ref, lng_ref, lnb_ref, ws_ref, bs_ref, wsp_ref, wout_ref,
                n2g_ref, wr_ref, br_ref,
                x1_ref, h2_ref, logit_ref, gated_scr):
    tm = x_ref.shape[0]
    y_attn = _dot(attn_ref[...], wap_ref[...])

    s = s_ref[0].astype(F32)
    mu = jnp.mean(s, axis=-1, keepdims=True)
    sc = s - mu
    var = jnp.mean(sc * sc, axis=-1, keepdims=True)
    v = ((sc * lax.rsqrt(var + EPS)) * lng_ref[...] + lnb_ref[...]).astype(BF16)
    n_chunks = tm // SGU_CHUNK
    for g in range(N_SGU_GROUPS):
        cols = slice(g * SGU_CHUNK, (g + 1) * SGU_CHUNK)
        v_g = jnp.concatenate([v[c * SGU_CHUNK:(c + 1) * SGU_CHUNK, cols] for c in range(n_chunks)], axis=1)
        mixed_g = _dot(ws_ref[g], v_g)
        for c in range(n_chunks):
            rows = slice(c * SGU_CHUNK, (c + 1) * SGU_CHUNK)
            mixed = mixed_g[:, c * SGU_CHUNK:(c + 1) * SGU_CHUNK] + bs_ref[g]
            gated_scr[rows, cols] = (u_ref[0, rows, cols].astype(F32) * mixed).astype(BF16)
    y_sgu = _dot(gated_scr[...], wsp_ref[...])

    y = ga_ref[0].astype(F32) * y_attn + gb_ref[...].astype(F32) * y_sgu
    x1 = x_ref[...] + mod_ref[0, 2:3, :] * _dot(y.astype(BF16), wout_ref[...])
    x1_ref[...] = x1

    h2 = x1 * lax.rsqrt(jnp.mean(x1 * x1, axis=-1, keepdims=True) + EPS) * n2g_ref[...]
    h2 = h2 * (1.0 + mod_ref[0, 4:5, :]) + mod_ref[0, 3:4, :]
    h2_hi = h2.astype(BF16)
    h2_ref[...] = h2_hi
    h2_lo = (h2 - h2_hi.astype(F32)).astype(BF16)
    parts = _dot(jnp.concatenate([h2_hi, h2_lo], axis=0), wr_ref[...])
    logits = ((parts[:tm, :ROUTER_LANES] + parts[:tm, ROUTER_LANES:])
              + (parts[tm:, :ROUTER_LANES] + parts[tm:, ROUTER_LANES:]) + br_ref[...])
    logit_ref[...] = logits.T


def _mix(attn, proj_b, gate_sgu, x2d, mod3, wap, lng, lnb, ws, bs_full, wsp, wout, n2g, wr, br, seq):
    t, d = x2d.shape
    tm = MIX_TILE
    tiles_per_batch = seq // tm
    const2 = lambda shape: pl.BlockSpec(shape, lambda i: (0, 0))
    const3 = lambda shape: pl.BlockSpec(shape, lambda i: (0, 0, 0))
    seg = lambda k: pl.BlockSpec((1, tm, d), lambda i, k=k: (k, i, 0))
    row = pl.BlockSpec((tm, d), lambda i: (i, 0))
    return pl.pallas_call(
        _mix_kernel,
        grid=(t // tm,),
        in_specs=[row, seg(0), seg(1), seg(2), row, row,
                  pl.BlockSpec((1, 6, d), lambda i: (i // tiles_per_batch, 0, 0)),
                  const2((d, d)), const2((1, d)), const2((1, d)),
                  const3((N_SGU_GROUPS, SGU_CHUNK, SGU_CHUNK)),
                  const3((N_SGU_GROUPS, SGU_CHUNK, SGU_CHUNK)),
                  const2((d, d)), const2((d, d)), const2((1, d)),
                  const2((d, 2 * ROUTER_LANES)), const2((1, ROUTER_LANES))],
        out_specs=[row, row, pl.BlockSpec((ROUTER_LANES, tm), lambda i: (0, i))],
        out_shape=[jax.ShapeDtypeStruct((t, d), F32), jax.ShapeDtypeStruct((t, d), BF16),
                   jax.ShapeDtypeStruct((ROUTER_LANES, t), F32)],
        scratch_shapes=[pltpu.VMEM((tm, d), BF16)],
        compiler_params=_params(("arbitrary",)),
        name="mix",
    )(attn, proj_b, proj_b, proj_b, gate_sgu, x2d, mod3, wap, lng, lnb, ws, bs_full, wsp, wout, n2g, wr, br)


def _route_kernel(logit_ref, comb_ref, combt_ref):
    z = logit_ref[:ROUTER_ROWS, :]
    row = lax.broadcasted_iota(jnp.int32, z.shape, 0)
    neg = jnp.float32(-jnp.inf)
    big = jnp.int32(ROUTER_LANES)

    def first_argmax(val, vmax):
        return jnp.min(jnp.where(val == vmax, row, big), axis=0, keepdims=True)

    is_group = (row >= N_EXPERTS) & (row < N_EXPERTS + N_GROUPS)
    gl = jnp.where(is_group, z, neg)
    gmax = jnp.max(gl, axis=0, keepdims=True)
    ge = jnp.exp(gl - gmax)
    gp = ge / jnp.sum(ge, axis=0, keepdims=True)
    gval = jnp.max(gp, axis=0, keepdims=True)
    gidx = first_argmax(gp, gval) - N_EXPERTS

    in_group = (row >= gidx * EXPERTS_PER_GROUP) & (row < (gidx + 1) * EXPERTS_PER_GROUP)
    el = jnp.where(in_group, z, neg)
    emax = jnp.max(el, axis=0, keepdims=True)
    ee = jnp.exp(el - emax)
    ep = ee / jnp.sum(ee, axis=0, keepdims=True)
    ep = jnp.where(in_group, ep, -1.0)
    ev0 = jnp.max(ep, axis=0, keepdims=True)
    ei0 = first_argmax(ep, ev0)
    ep_rest = jnp.where(row == ei0, -1.0, ep)
    ev1 = jnp.max(ep_rest, axis=0, keepdims=True)
    ei1 = first_argmax(ep_rest, ev1)
    denom = ev0 + ev1
    weights = jnp.where(row == ei0, ev0 / denom * gval, jnp.where(row == ei1, ev1 / denom * gval, 0.0))
    tm = z.shape[1]
    combt = jnp.concatenate(
        [weights, jnp.zeros((GROUP_ID_LANE - ROUTER_ROWS, tm), F32),
         jnp.broadcast_to(gidx.astype(F32), (8, tm)) * (lax.broadcasted_iota(jnp.int32, (8, tm), 0) == 0),
         jnp.zeros((ROUTER_LANES - GROUP_ID_LANE - 8, tm), F32)], axis=0)
    combt_ref[...] = combt
    comb_ref[...] = combt.T


def _route(logits_t):
    t = logits_t.shape[1]
    tm = min(ROUTE_TILE, t)
    spec_t = pl.BlockSpec((ROUTER_LANES, tm), lambda i: (0, i))
    return pl.pallas_call(
        _route_kernel, grid=(t // tm,), in_specs=[spec_t],
        out_specs=[pl.BlockSpec((tm, ROUTER_LANES), lambda i: (i, 0)), spec_t],
        out_shape=[jax.ShapeDtypeStruct((t, ROUTER_LANES), F32), jax.ShapeDtypeStruct((ROUTER_LANES, t), F32)],
        compiler_params=_params(("arbitrary",)), name="route",
    )(logits_t)


def _moe_kernel(h2_ref, comb_ref, combt_ref, wgu_ref, wd_ref, x1_ref, mod_ref, fg_ref, out_ref,
                onehot_scr, xs_scr, wl_scr, outc_scr, meta_ref):
    j = pl.program_id(1)
    tt = h2_ref.shape[0]
    rc = onehot_scr.shape[0]

    @pl.when(j == 0)
    def _():
        ct = combt_ref[...]
        gid = ct[GROUP_ID_LANE:GROUP_ID_LANE + 1, :]
        erow = lax.broadcasted_iota(jnp.int32, ct.shape, 0)
        first = (erow < N_EXPERTS) & ((erow & (EXPERTS_PER_GROUP - 1)) < MOE_EXPERTS_PER_STEP)
        second = (erow < N_EXPERTS) & ((erow & (EXPERTS_PER_GROUP - 1)) >= MOE_EXPERTS_PER_STEP)
        need_first = jnp.max(jnp.where(first, ct, 0.0), axis=0, keepdims=True) > 0.0
        need_second = jnp.max(jnp.where(second, ct, 0.0), axis=0, keepdims=True) > 0.0
        cls = jnp.where(need_first, jnp.where(need_second, 1.0, 0.0), 2.0)
        key = gid * MOE_CLASSES + cls
        sub = lax.broadcasted_iota(jnp.int32, (N_GROUPS * MOE_CLASSES + 4, tt), 0).astype(F32)
        member = jnp.where(key == sub, 1.0, 0.0)
        before = jnp.where(lax.broadcasted_iota(jnp.int32, (tt, tt), 0)
                           < lax.broadcasted_iota(jnp.int32, (tt, tt), 1), 1.0, 0.0).astype(BF16)
        rank = _dot(member.astype(BF16), before)
        pos = jnp.zeros((1, tt), F32)
        off = jnp.int32(0)
        n_steps = pl.num_programs(1)
        windows = lambda n_rows: (n_rows + (MOE_CHUNK - 1)) // MOE_CHUNK
        for gg in range(N_GROUPS):
            start = off
            counts = []
            for c in range(MOE_CLASSES):
                r = gg * MOE_CLASSES + c
                counts.append(jnp.sum(member[r:r + 1, :]).astype(jnp.int32))
                pos = pos + member[r:r + 1, :] * (start.astype(F32) + rank[r:r + 1, :])
                start = start + counts[c]
            second_start = (off + counts[0]) // MOE_ROW_ALIGN * MOE_ROW_ALIGN
            meta_ref[2 * gg] = off
            meta_ref[n_steps + 2 * gg] = windows(counts[0] + counts[1])
            meta_ref[2 * gg + 1] = second_start
            meta_ref[n_steps + 2 * gg + 1] = windows(start - second_start)
            off = (start + (MOE_ROW_ALIGN - 1)) // MOE_ROW_ALIGN * MOE_ROW_ALIGN
        row = lax.broadcasted_iota(jnp.int32, (rc, tt), 0).astype(F32)
        onehot = jnp.where(row == pos, 1.0, 0.0).astype(BF16)
        onehot_scr[...] = onehot
        xs_scr[:rc, :] = _dot(onehot, h2_ref[...]).astype(BF16)
        xs_scr[rc:, :] = jnp.zeros((xs_scr.shape[0] - rc, xs_scr.shape[1]), BF16)
        comb = comb_ref[...]
        c_hi = comb.astype(BF16)
        c_lo = (comb - c_hi.astype(F32)).astype(BF16)
        wl2 = _dot(onehot, jnp.concatenate([c_hi, c_lo], axis=1))
        wl_scr[:rc, :] = wl2[:, :ROUTER_LANES] + wl2[:, ROUTER_LANES:]
        wl_scr[rc:, :] = jnp.zeros((wl_scr.shape[0] - rc, ROUTER_LANES), F32)
        outc_scr[...] = jnp.zeros_like(outc_scr)

    off_g = meta_ref[j]
    n_chunks_g = meta_ref[pl.num_programs(1) + j]

    def experts(r0, rows):
        xs = xs_scr[pl.ds(r0, rows), :]
        wl = wl_scr[pl.ds(r0, rows), :]
        lane = lax.broadcasted_iota(jnp.int32, wl.shape, 1)
        acts = []
        for e in range(MOE_EXPERTS_PER_STEP):
            gu = _dot(xs, wgu_ref[e])
            gate = gu[:, :D_EXPERT]
            up = gu[:, D_EXPERT:]
            w = jnp.sum(jnp.where(lane == j * MOE_EXPERTS_PER_STEP + e, wl, 0.0), axis=-1, keepdims=True)
            acts.append((gate * _sigmoid(gate) * up * w).astype(BF16))
        wd_all = wd_ref[...].reshape(MOE_EXPERTS_PER_STEP * D_EXPERT, D_MODEL)
        outc_scr[pl.ds(r0, rows), :] += _dot(jnp.concatenate(acts, axis=1), wd_all)

    def chunk_pair(k, carry):
        experts(pl.multiple_of(off_g + k * (2 * MOE_CHUNK), MOE_ROW_ALIGN), 2 * MOE_CHUNK)
        return carry

    lax.fori_loop(0, n_chunks_g // 2, chunk_pair, 0)

    @pl.when(n_chunks_g % 2 == 1)
    def _():
        experts(pl.multiple_of(off_g + (n_chunks_g - 1) * MOE_CHUNK, MOE_ROW_ALIGN), MOE_CHUNK)

    @pl.when(j == pl.num_programs(1) - 1)
    def _():
        y = lax.dot_general(onehot_scr[...], outc_scr[:rc, :].astype(BF16), (((0,), (0,)), ((), ())),
                            preferred_element_type=F32)
        x2 = x1_ref[...] + mod_ref[0, 5:6, :] * y
        out_ref[...] = x2 * lax.rsqrt(jnp.mean(x2 * x2, axis=-1, keepdims=True) + EPS) * fg_ref[...]


def _moe(h2, comb, combt, wgu, wd, x1, mod3, final_g, seq):
    t, d = x1.shape
    tm = MOE_TILE
    tiles_per_batch = seq // tm
    n_steps = N_EXPERTS // MOE_EXPERTS_PER_STEP
    assert EXPERTS_PER_GROUP == 2 * MOE_EXPERTS_PER_STEP, "row classes assume two expert steps per group"
    compact_rows = tm + N_GROUPS * MOE_ROW_ALIGN
    buffer_rows = compact_rows + MOE_CHUNK
    row = pl.BlockSpec((tm, d), lambda i, j: (i, 0))
    return pl.pallas_call(
        _moe_kernel,
        grid=(t // tm, n_steps),
        in_specs=[row,
                  pl.BlockSpec((tm, ROUTER_LANES), lambda i, j: (i, 0)),
                  pl.BlockSpec((ROUTER_LANES, tm), lambda i, j: (0, i)),
                  pl.BlockSpec((MOE_EXPERTS_PER_STEP, d, 2 * D_EXPERT), lambda i, j: (j, 0, 0)),
                  pl.BlockSpec((MOE_EXPERTS_PER_STEP, D_EXPERT, d), lambda i, j: (j, 0, 0)),
                  row,
                  pl.BlockSpec((1, 6, d), lambda i, j: (i // tiles_per_batch, 0, 0)),
                  pl.BlockSpec((1, d), lambda i, j: (0, 0))],
        out_specs=row,
        out_shape=jax.ShapeDtypeStruct((t, d), F32),
        scratch_shapes=[pltpu.VMEM((compact_rows, tm), BF16),
                        pltpu.VMEM((buffer_rows, d), BF16),
                        pltpu.VMEM((buffer_rows, ROUTER_LANES), F32),
                        pltpu.VMEM((buffer_rows, d), F32),
                        pltpu.SMEM((2 * n_steps,), jnp.int32)],
        compiler_params=_params(("arbitrary", "arbitrary")),
        name="moe",
    )(h2, comb, combt, wgu, wd, x1, mod3, final_g.reshape(1, d))


def _router_weights(w_rg, b_rg, w_re, b_re):
    d = w_rg.shape[0]
    pad = ROUTER_LANES - N_EXPERTS - N_GROUPS
    w = jnp.concatenate([w_re, w_rg, jnp.zeros((d, pad), F32)], axis=1)
    b = jnp.concatenate([b_re, b_rg, jnp.zeros((pad,), F32)]).reshape(1, ROUTER_LANES)
    w_hi = w.astype(BF16)
    w_lo = (w - w_hi.astype(F32)).astype(BF16)
    return jnp.concatenate([w_hi, w_lo], axis=1), b


def kernel(x, c, w_ada, b_ada, norm1_g, w_in, lambda_q1, lambda_k1, lambda_q2, lambda_k2, subln_g, w_attn_proj, sgu_ln_g, sgu_ln_b, sgu_w_s, sgu_b_s, w_sgu_proj, w_out, norm2_g, w_router_group, b_router_group, w_router_expert, b_router_expert, w_expert_gate_up, w_expert_down, final_g):
    batch, seq, d = x.shape
    assert w_ada.shape[0] == 1, "single-layer trunk"
    x2d = x.reshape(batch * seq, d)

    mod, lam = _ada(c, w_ada[0], b_ada[0], lambda_q1[0], lambda_k1[0], lambda_q2[0], lambda_k2[0])
    mod3 = mod.reshape(batch, 6, d)

    proj_a, proj_b, gate_sgu = _in_proj(x2d, mod3, norm1_g[0], w_in[0].astype(BF16), seq)
    attn = _attention(proj_a, lam, subln_g[0], batch, seq)

    bs_full = jnp.broadcast_to(sgu_b_s[0][:, :, None], (N_SGU_GROUPS, SGU_CHUNK, SGU_CHUNK))
    wr, br = _router_weights(w_router_group[0], b_router_group[0], w_router_expert[0], b_router_expert[0])
    x1, h2, logits = _mix(attn, proj_b, gate_sgu, x2d, mod3, w_attn_proj[0].astype(BF16),
                          sgu_ln_g[0].reshape(1, d), sgu_ln_b[0].reshape(1, d),
                          sgu_w_s[0].astype(BF16), bs_full, w_sgu_proj[0].astype(BF16),
                          w_out[0].astype(BF16), norm2_g[0].reshape(1, d), wr, br, seq)
    comb, combt = _route(logits)
    out = _moe(h2, comb, combt, w_expert_gate_up[0].astype(BF16), w_expert_down[0].astype(BF16),
               x1, mod3, final_g, seq)
    return out.reshape(batch, seq, d)
```

```python
import functools
import math

import jax
import jax.numpy as jnp
import numpy as np
from jax import lax
from jax.experimental import pallas as pl
from jax.experimental.pallas import tpu as pltpu

D_MODEL = 1024
N_HEADS = 8
HEAD_DIM = 64
HEAD_WIDTH = 2 * HEAD_DIM
N_SGU_GROUPS = 8
SGU_CHUNK = 128
N_GROUPS = 4
EXPERTS_PER_GROUP = 8
N_EXPERTS = N_GROUPS * EXPERTS_PER_GROUP
D_EXPERT = 256
N_SEGMENTS = 7
EPS = 1e-6
LAMBDA_INIT = 0.8 - 0.6 * math.exp(-0.3 * 0)
ALIBI_SLOPES = np.array([2.0 ** (-8.0 * (h + 1) / N_HEADS) for h in range(N_HEADS)], dtype=np.float32)
ROUTER_LANES = 128
ROUTE_TILE = 4096
ROUTER_ROWS = 40
GROUP_ID_LANE = 64
MOE_TILE = 1024
MOE_CHUNK = 128
MOE_CLASSES = 3
MOE_ROW_ALIGN = 16
MOE_EXPERTS_PER_STEP = 4
LOG2_E = math.log2(math.e)
IN_PROJ_TILE = 1024
IN_PROJ_STEPS = 3
IN_PROJ_SUB = 256
MIX_TILE = 512
ATTN_ROWS = 256
ATTN_KEY_TILE = 1024
ATTN_VT_ROWS = HEAD_WIDTH + 16
ATTN_L_MIN = 2.0 ** -60
ATTN_L_MAX = 2.0 ** 100

F32 = jnp.float32
BF16 = jnp.bfloat16
VMEM_LIMIT_BYTES = 56 * 1024 * 1024


def _params(semantics):
    return pltpu.CompilerParams(dimension_semantics=semantics, vmem_limit_bytes=VMEM_LIMIT_BYTES)


def _dot(a, b):
    return jnp.dot(a, b, preferred_element_type=F32)


def _sigmoid(x):
    return 1.0 / (1.0 + jnp.exp(-x))


def _gelu_tanh(x):
    return 0.5 * x * (1.0 + jnp.tanh(math.sqrt(2.0 / math.pi) * (x + 0.044715 * (x * x * x))))


def _ada_kernel(c_ref, w_ref, b_ref, lq1_ref, lk1_ref, lq2_ref, lk2_ref, mod_ref, lam_ref):
    c = c_ref[...]
    act = c * _sigmoid(c)
    mod_ref[...] = jnp.dot(act, w_ref[...], preferred_element_type=F32,
                           precision=lax.Precision.HIGHEST) + b_ref[...]
    d1 = jnp.sum(lq1_ref[...] * lk1_ref[...], axis=-1, keepdims=True)
    d2 = jnp.sum(lq2_ref[...] * lk2_ref[...], axis=-1, keepdims=True)
    lam_ref[...] = jnp.exp(d1) - jnp.exp(d2) + LAMBDA_INIT


def _ada(c, w_ada, b_ada, lq1, lk1, lq2, lk2):
    batch, d = c.shape
    n = w_ada.shape[1]
    tn = 1024
    vec = pl.BlockSpec((1, HEAD_DIM), lambda j: (0, 0))
    return pl.pallas_call(
        _ada_kernel,
        grid=(n // tn,),
        in_specs=[pl.BlockSpec((batch, d), lambda j: (0, 0)),
                  pl.BlockSpec((d, tn), lambda j: (0, j)),
                  pl.BlockSpec((1, tn), lambda j: (0, j)),
                  vec, vec, vec, vec],
        out_specs=[pl.BlockSpec((batch, tn), lambda j: (0, j)),
                   pl.BlockSpec((1, 1), lambda j: (0, 0))],
        out_shape=[jax.ShapeDtypeStruct((batch, n), F32), jax.ShapeDtypeStruct((1, 1), F32)],
        compiler_params=_params(("arbitrary",)),
        name="ada",
    )(c, w_ada, b_ada.reshape(1, n), lq1.reshape(1, -1), lk1.reshape(1, -1),
      lq2.reshape(1, -1), lk2.reshape(1, -1))


def _in_proj_kernel(x_ref, mod_ref, g_ref, w_ref, oa_ref, ob_ref, oc_ref, h_scr):
    d = h_scr.shape[1]
    x = x_ref[...]
    y = x * lax.rsqrt(jnp.mean(x * x, axis=-1, keepdims=True) + EPS) * g_ref[...]
    h_scr[...] = (y * (1.0 + mod_ref[0, 1:2, :]) + mod_ref[0, 0:1, :]).astype(BF16)

    def w(segment, cs):
        return w_ref[:, segment * d + cs.start:segment * d + cs.stop]

    def project(step, epilogue_a, epilogue_b, gate_subtiles):
        for n in range(d // IN_PROJ_SUB):
            cs = slice(n * IN_PROJ_SUB, (n + 1) * IN_PROJ_SUB)
            oa_ref[step, :, cs] = epilogue_a(_dot(h_scr[...], w(step, cs))).astype(BF16)
            ob_ref[step, :, cs] = epilogue_b(_dot(h_scr[...], w(IN_PROJ_STEPS + step, cs))).astype(BF16)
            if n < len(gate_subtiles):
                gs = slice(gate_subtiles[n] * IN_PROJ_SUB, (gate_subtiles[n] + 1) * IN_PROJ_SUB)
                oc_ref[:, gs] = _sigmoid(_dot(h_scr[...], w(N_SEGMENTS - 1, gs))).astype(BF16)

    project(0, lambda acc: acc * (HEAD_DIM ** -0.5 * LOG2_E), _gelu_tanh, (0, 1))
    project(1, lambda acc: acc, _gelu_tanh, (2,))
    project(2, lambda acc: acc, _sigmoid, (3,))


def _in_proj(x2d, mod3, norm_g, w_in_bf, seq):
    t, d = x2d.shape
    tm = IN_PROJ_TILE
    tiles_per_batch = seq // tm
    assert d // IN_PROJ_SUB == 4 and N_SEGMENTS == 2 * IN_PROJ_STEPS + 1
    stack = pl.BlockSpec((IN_PROJ_STEPS, tm, d), lambda i: (0, i, 0))
    return pl.pallas_call(
        _in_proj_kernel,
        grid=(t // tm,),
        in_specs=[pl.BlockSpec((tm, d), lambda i: (i, 0)),
                  pl.BlockSpec((1, 6, d), lambda i: (i // tiles_per_batch, 0, 0)),
                  pl.BlockSpec((1, d), lambda i: (0, 0)),
                  pl.BlockSpec((d, N_SEGMENTS * d), lambda i: (0, 0), pipeline_mode=pl.Buffered(1))],
        out_specs=[stack, stack, pl.BlockSpec((tm, d), lambda i: (i, 0))],
        out_shape=[jax.ShapeDtypeStruct((IN_PROJ_STEPS, t, d), BF16), jax.ShapeDtypeStruct((IN_PROJ_STEPS, t, d), BF16),
                   jax.ShapeDtypeStruct((t, d), BF16)],
        scratch_shapes=[pltpu.VMEM((tm, d), BF16)],
        compiler_params=_params(("arbitrary",)),
        name="in_proj",
    )(x2d, mod3, norm_g.reshape(1, d), w_in_bf)


def _attn_kernel(slopes_ref, lam_ref, q_ref, k_ref, v_ref, subg_ref, subg_col_ref, o_ref,
                 bias_scr, p_scr, vt_scr, *, rows, seq):
    h = pl.program_id(0)
    b = pl.program_id(1)
    nq = seq // rows
    contract_last = (((1,), (1,)), ((), ()))

    @pl.when(b == 0)
    def _():
        slope = -LOG2_E * slopes_ref[h]
        for qb in range(nq):
            kpos = lax.broadcasted_iota(jnp.int32, (seq, rows), 0)
            qpos = qb * rows + lax.broadcasted_iota(jnp.int32, (seq, rows), 1)
            bias_scr[qb] = slope * jnp.abs(qpos - kpos).astype(F32)
        sub = lax.broadcasted_iota(jnp.int32, (ATTN_VT_ROWS - HEAD_WIDTH, seq), 0)
        vt_scr[HEAD_WIDTH:, :] = jnp.where(sub == 0, 1.0, 0.0).astype(BF16)

    vt_scr[:HEAD_WIDTH, :] = v_ref[0].astype(F32).T.astype(BF16)
    lam = lam_ref[0, 0]

    def masked_q(r0):
        q = q_ref[0, pl.ds(r0, rows), :]
        lane = lax.broadcasted_iota(jnp.int32, q.shape, 1)
        zero = jnp.zeros_like(q)
        return jnp.where(lane < HEAD_DIM, q, zero), jnp.where(lane >= HEAD_DIM, q, zero)

    def fast_block(qb, n_bad):
        r0 = pl.multiple_of(qb * rows, rows)
        q0, q1 = masked_q(r0)
        qq = jnp.concatenate([q0, q1], axis=0)
        for c in range(seq // ATTN_KEY_TILE):
            ks = slice(c * ATTN_KEY_TILE, (c + 1) * ATTN_KEY_TILE)
            st = lax.dot_general(k_ref[0, ks, :], qq, contract_last, preferred_element_type=F32)
            bias = bias_scr[qb, ks, :]
            p_scr[ks, :] = jnp.exp2(st + jnp.concatenate([bias, bias], axis=1)).astype(BF16)
        oo = _dot(vt_scr[...], p_scr[...])
        l0 = oo[HEAD_WIDTH:HEAD_WIDTH + 1, :rows]
        l1 = oo[HEAD_WIDTH:HEAD_WIDTH + 1, rows:]
        ot = oo[:HEAD_WIDTH, :rows] / l0 - lam * (oo[:HEAD_WIDTH, rows:] / l1)
        ot = ot * lax.rsqrt(jnp.mean(ot * ot, axis=0, keepdims=True) + EPS)
        ot = ot * subg_col_ref[...] * (1.0 - LAMBDA_INIT)
        o_ref[pl.ds(r0, rows), :] = ot.T.astype(BF16)
        ok = ((l0 >= ATTN_L_MIN) & (l0 <= ATTN_L_MAX)) & ((l1 >= ATTN_L_MIN) & (l1 <= ATTN_L_MAX))
        return n_bad + jnp.where(ok, 0.0, 1.0)

    n_bad = lax.fori_loop(0, nq, fast_block, jnp.zeros((1, rows), F32), unroll=True)

    @pl.when(jnp.sum(n_bad) > 0.0)
    def _():
        subg = subg_ref[...]

        def safe_block(qb, carry):
            r0 = pl.multiple_of(qb * rows, rows)
            outs = []
            for qm in masked_q(r0):
                s = lax.dot_general(qm, k_ref[0], contract_last, preferred_element_type=F32) + bias_scr[qb].T
                p = jnp.exp2(s - jnp.max(s, axis=-1, keepdims=True))
                outs.append(_dot(p.astype(BF16), v_ref[0]) / jnp.sum(p, axis=-1, keepdims=True))
            o = outs[0] - lam * outs[1]
            o = o * lax.rsqrt(jnp.mean(o * o, axis=-1, keepdims=True) + EPS)
            o_ref[pl.ds(r0, rows), :] = (o * subg * (1.0 - LAMBDA_INIT)).astype(BF16)
            return carry

        lax.fori_loop(0, nq, safe_block, 0)


def _attention(qkv, lam, subln_g, batch, seq):
    t = batch * seq
    rows = ATTN_ROWS
    kernel = functools.partial(_attn_kernel, rows=rows, seq=seq)
    smem = pl.BlockSpec(memory_space=pltpu.SMEM)
    seg = lambda k: pl.BlockSpec((1, seq, HEAD_WIDTH), lambda h, b, k=k: (k, b, h))
    return pl.pallas_call(
        kernel,
        grid=(N_HEADS, batch),
        in_specs=[smem, smem, seg(0), seg(1), seg(2),
                  pl.BlockSpec((1, HEAD_WIDTH), lambda h, b: (0, 0)),
                  pl.BlockSpec((HEAD_WIDTH, 1), lambda h, b: (0, 0))],
        out_specs=pl.BlockSpec((seq, HEAD_WIDTH), lambda h, b: (b, h)),
        out_shape=jax.ShapeDtypeStruct((t, N_HEADS * HEAD_WIDTH), BF16),
        scratch_shapes=[pltpu.VMEM((seq // rows, seq, rows), F32),
                        pltpu.VMEM((seq, 2 * rows), BF16),
                        pltpu.VMEM((ATTN_VT_ROWS, seq), BF16)],
        compiler_params=_params(("arbitrary", "arbitrary")),
        name="diff_attn",
    )(jnp.asarray(ALIBI_SLOPES), lam, qkv, qkv, qkv, subln_g.reshape(1, HEAD_WIDTH),
      subln_g.reshape(HEAD_WIDTH, 1))


def _mix_kernel(attn_ref, u_ref, s_ref, ga_ref, gb_ref, x_ref, mod_ref,
                wap_ref, lng_ref, lnb_ref, ws_ref, bs_ref, wsp_ref, wout_ref,
                n2g_ref, wr_ref, br_ref,
                x1_ref, h2_ref, logit_ref, gated_scr):
    tm = x_ref.shape[0]
    y_attn = _dot(attn_ref[...], wap_ref[...])

    s = s_ref[0].astype(F32)
    mu = jnp.mean(s, axis=-1, keepdims=True)
    sc = s - mu
    var = jnp.mean(sc * sc, axis=-1, keepdims=True)
    v = ((sc * lax.rsqrt(var + EPS)) * lng_ref[...] + lnb_ref[...]).astype(BF16)
    n_chunks = tm // SGU_CHUNK
    for g in range(N_SGU_GROUPS):
        cols = slice(g * SGU_CHUNK, (g + 1) * SGU_CHUNK)
        v_g = jnp.concatenate([v[c * SGU_CHUNK:(c + 1) * SGU_CHUNK, cols] for c in range(n_chunks)], axis=1)
        mixed_g = _dot(ws_ref[g], v_g)
        for c in range(n_chunks):
            rows = slice(c * SGU_CHUNK, (c + 1) * SGU_CHUNK)
            mixed = mixed_g[:, c * SGU_CHUNK:(c + 1) * SGU_CHUNK] + bs_ref[g]
            gated_scr[rows, cols] = (u_ref[0, rows, cols].astype(F32) * mixed).astype(BF16)
    y_sgu = _dot(gated_scr[...], wsp_ref[...])

    y = ga_ref[0].astype(F32) * y_attn + gb_ref[...].astype(F32) * y_sgu
    x1 = x_ref[...] + mod_ref[0, 2:3, :] * _dot(y.astype(BF16), wout_ref[...])
    x1_ref[...] = x1

    h2 = x1 * lax.rsqrt(jnp.mean(x1 * x1, axis=-1, keepdims=True) + EPS) * n2g_ref[...]
    h2 = h2 * (1.0 + mod_ref[0, 4:5, :]) + mod_ref[0, 3:4, :]
    h2_hi = h2.astype(BF16)
    h2_ref[...] = h2_hi
    h2_lo = (h2 - h2_hi.astype(F32)).astype(BF16)
    parts = _dot(jnp.concatenate([h2_hi, h2_lo], axis=0), wr_ref[...])
    logits = ((parts[:tm, :ROUTER_LANES] + parts[:tm, ROUTER_LANES:])
              + (parts[tm:, :ROUTER_LANES] + parts[tm:, ROUTER_LANES:]) + br_ref[...])
    logit_ref[...] = logits.T


def _mix(attn, proj_b, gate_sgu, x2d, mod3, wap, lng, lnb, ws, bs_full, wsp, wout, n2g, wr, br, seq):
    t, d = x2d.shape
    tm = MIX_TILE
    tiles_per_batch = seq // tm
    const2 = lambda shape: pl.BlockSpec(shape, lambda i: (0, 0))
    const3 = lambda shape: pl.BlockSpec(shape, lambda i: (0, 0, 0))
    seg = lambda k: pl.BlockSpec((1, tm, d), lambda i, k=k: (k, i, 0))
    row = pl.BlockSpec((tm, d), lambda i: (i, 0))
    return pl.pallas_call(
        _mix_kernel,
        grid=(t // tm,),
        in_specs=[row, seg(0), seg(1), seg(2), row, row,
                  pl.BlockSpec((1, 6, d), lambda i: (i // tiles_per_batch, 0, 0)),
                  const2((d, d)), const2((1, d)), const2((1, d)),
                  const3((N_SGU_GROUPS, SGU_CHUNK, SGU_CHUNK)),
                  const3((N_SGU_GROUPS, SGU_CHUNK, SGU_CHUNK)),
                  const2((d, d)), const2((d, d)), const2((1, d)),
                  const2((d, 2 * ROUTER_LANES)), const2((1, ROUTER_LANES))],
        out_specs=[row, row, pl.BlockSpec((ROUTER_LANES, tm), lambda i: (0, i))],
        out_shape=[jax.ShapeDtypeStruct((t, d), F32), jax.ShapeDtypeStruct((t, d), BF16),
                   jax.ShapeDtypeStruct((ROUTER_LANES, t), F32)],
        scratch_shapes=[pltpu.VMEM((tm, d), BF16)],
        compiler_params=_params(("arbitrary",)),
        name="mix",
    )(attn, proj_b, proj_b, proj_b, gate_sgu, x2d, mod3, wap, lng, lnb, ws, bs_full, wsp, wout, n2g, wr, br)


def _route_kernel(logit_ref, comb_ref, combt_ref):
    z = logit_ref[:ROUTER_ROWS, :]
    row = lax.broadcasted_iota(jnp.int32, z.shape, 0)
    neg = jnp.float32(-jnp.inf)
    big = jnp.int32(ROUTER_LANES)

    def first_argmax(val, vmax):
        return jnp.min(jnp.where(val == vmax, row, big), axis=0, keepdims=True)

    is_group = (row >= N_EXPERTS) & (row < N_EXPERTS + N_GROUPS)
    gl = jnp.where(is_group, z, neg)
    gmax = jnp.max(gl, axis=0, keepdims=True)
    ge = jnp.exp(gl - gmax)
    gp = ge / jnp.sum(ge, axis=0, keepdims=True)
    gval = jnp.max(gp, axis=0, keepdims=True)
    gidx = first_argmax(gp, gval) - N_EXPERTS

    in_group = (row >= gidx * EXPERTS_PER_GROUP) & (row < (gidx + 1) * EXPERTS_PER_GROUP)
    el = jnp.where(in_group, z, neg)
    emax = jnp.max(el, axis=0, keepdims=True)
    ee = jnp.exp(el - emax)
    ep = ee / jnp.sum(ee, axis=0, keepdims=True)
    ep = jnp.where(in_group, ep, -1.0)
    ev0 = jnp.max(ep, axis=0, keepdims=True)
    ei0 = first_argmax(ep, ev0)
    ep_rest = jnp.where(row == ei0, -1.0, ep)
    ev1 = jnp.max(ep_rest, axis=0, keepdims=True)
    ei1 = first_argmax(ep_rest, ev1)
    denom = ev0 + ev1
    weights = jnp.where(row == ei0, ev0 / denom * gval, jnp.where(row == ei1, ev1 / denom * gval, 0.0))
    tm = z.shape[1]
    combt = jnp.concatenate(
        [weights, jnp.zeros((GROUP_ID_LANE - ROUTER_ROWS, tm), F32),
         jnp.broadcast_to(gidx.astype(F32), (8, tm)) * (lax.broadcasted_iota(jnp.int32, (8, tm), 0) == 0),
         jnp.zeros((ROUTER_LANES - GROUP_ID_LANE - 8, tm), F32)], axis=0)
    combt_ref[...] = combt
    comb_ref[...] = combt.T


def _route(logits_t):
    t = logits_t.shape[1]
    tm = min(ROUTE_TILE, t)
    spec_t = pl.BlockSpec((ROUTER_LANES, tm), lambda i: (0, i))
    return pl.pallas_call(
        _route_kernel, grid=(t // tm,), in_specs=[spec_t],
        out_specs=[pl.BlockSpec((tm, ROUTER_LANES), lambda i: (i, 0)), spec_t],
        out_shape=[jax.ShapeDtypeStruct((t, ROUTER_LANES), F32), jax.ShapeDtypeStruct((ROUTER_LANES, t), F32)],
        compiler_params=_params(("arbitrary",)), name="route",
    )(logits_t)


def _moe_kernel(h2_ref, comb_ref, combt_ref, wgu_ref, wd_ref, x1_ref, mod_ref, fg_ref, out_ref,
                onehot_scr, xs_scr, wl_scr, outc_scr, meta_ref):
    j = pl.program_id(1)
    tt = h2_ref.shape[0]
    rc = onehot_scr.shape[0]

    @pl.when(j == 0)
    def _():
        ct = combt_ref[...]
        gid = ct[GROUP_ID_LANE:GROUP_ID_LANE + 1, :]
        erow = lax.broadcasted_iota(jnp.int32, ct.shape, 0)
        first = (erow < N_EXPERTS) & ((erow & (EXPERTS_PER_GROUP - 1)) < MOE_EXPERTS_PER_STEP)
        second = (erow < N_EXPERTS) & ((erow & (EXPERTS_PER_GROUP - 1)) >= MOE_EXPERTS_PER_STEP)
        need_first = jnp.max(jnp.where(first, ct, 0.0), axis=0, keepdims=True) > 0.0
        need_second = jnp.max(jnp.where(second, ct, 0.0), axis=0, keepdims=True) > 0.0
        cls = jnp.where(need_first, jnp.where(need_second, 1.0, 0.0), 2.0)
        key = gid * MOE_CLASSES + cls
        sub = lax.broadcasted_iota(jnp.int32, (N_GROUPS * MOE_CLASSES + 4, tt), 0).astype(F32)
        member = jnp.where(key == sub, 1.0, 0.0)
        before = jnp.where(lax.broadcasted_iota(jnp.int32, (tt, tt), 0)
                           < lax.broadcasted_iota(jnp.int32, (tt, tt), 1), 1.0, 0.0).astype(BF16)
        rank = _dot(member.astype(BF16), before)
        pos = jnp.zeros((1, tt), F32)
        off = jnp.int32(0)
        n_steps = pl.num_programs(1)
        windows = lambda n_rows: (n_rows + (MOE_CHUNK - 1)) // MOE_CHUNK
        for gg in range(N_GROUPS):
            start = off
            counts = []
            for c in range(MOE_CLASSES):
                r = gg * MOE_CLASSES + c
                counts.append(jnp.sum(member[r:r + 1, :]).astype(jnp.int32))
                pos = pos + member[r:r + 1, :] * (start.astype(F32) + rank[r:r + 1, :])
                start = start + counts[c]
            second_start = (off + counts[0]) // MOE_ROW_ALIGN * MOE_ROW_ALIGN
            meta_ref[2 * gg] = off
            meta_ref[n_steps + 2 * gg] = windows(counts[0] + counts[1])
            meta_ref[2 * gg + 1] = second_start
            meta_ref[n_steps + 2 * gg + 1] = windows(start - second_start)
            off = (start + (MOE_ROW_ALIGN - 1)) // MOE_ROW_ALIGN * MOE_ROW_ALIGN
        row = lax.broadcasted_iota(jnp.int32, (rc, tt), 0).astype(F32)
        onehot = jnp.where(row == pos, 1.0, 0.0).astype(BF16)
        onehot_scr[...] = onehot
        xs_scr[:rc, :] = _dot(onehot, h2_ref[...]).astype(BF16)
        xs_scr[rc:, :] = jnp.zeros((xs_scr.shape[0] - rc, xs_scr.shape[1]), BF16)
        comb = comb_ref[...]
        c_hi = comb.astype(BF16)
        c_lo = (comb - c_hi.astype(F32)).astype(BF16)
        wl2 = _dot(onehot, jnp.concatenate([c_hi, c_lo], axis=1))
        wl_scr[:rc, :] = wl2[:, :ROUTER_LANES] + wl2[:, ROUTER_LANES:]
        wl_scr[rc:, :] = jnp.zeros((wl_scr.shape[0] - rc, ROUTER_LANES), F32)
        outc_scr[...] = jnp.zeros_like(outc_scr)

    off_g = meta_ref[j]
    n_chunks_g = meta_ref[pl.num_programs(1) + j]

    def experts(r0, rows):
        xs = xs_scr[pl.ds(r0, rows), :]
        wl = wl_scr[pl.ds(r0, rows), :]
        lane = lax.broadcasted_iota(jnp.int32, wl.shape, 1)
        acts = []
        for e in range(MOE_EXPERTS_PER_STEP):
            gu = _dot(xs, wgu_ref[e])
            gate = gu[:, :D_EXPERT]
            up = gu[:, D_EXPERT:]
            w = jnp.sum(jnp.where(lane == j * MOE_EXPERTS_PER_STEP + e, wl, 0.0), axis=-1, keepdims=True)
            acts.append((gate * _sigmoid(gate) * up * w).astype(BF16))
        wd_all = wd_ref[...].reshape(MOE_EXPERTS_PER_STEP * D_EXPERT, D_MODEL)
        outc_scr[pl.ds(r0, rows), :] += _dot(jnp.concatenate(acts, axis=1), wd_all)

    def chunk_pair(k, carry):
        experts(pl.multiple_of(off_g + k * (2 * MOE_CHUNK), MOE_ROW_ALIGN), 2 * MOE_CHUNK)
        return carry

    lax.fori_loop(0, n_chunks_g // 2, chunk_pair, 0)

    @pl.when(n_chunks_g % 2 == 1)
    def _():
        experts(pl.multiple_of(off_g + (n_chunks_g - 1) * MOE_CHUNK, MOE_ROW_ALIGN), MOE_CHUNK)

    @pl.when(j == pl.num_programs(1) - 1)
    def _():
        y = lax.dot_general(onehot_scr[...], outc_scr[:rc, :].astype(BF16), (((0,), (0,)), ((), ())),
                            preferred_element_type=F32)
        x2 = x1_ref[...] + mod_ref[0, 5:6, :] * y
        out_ref[...] = x2 * lax.rsqrt(jnp.mean(x2 * x2, axis=-1, keepdims=True) + EPS) * fg_ref[...]


def _moe(h2, comb, combt, wgu, wd, x1, mod3, final_g, seq):
    t, d = x1.shape
    tm = MOE_TILE
    tiles_per_batch = seq // tm
    n_steps = N_EXPERTS // MOE_EXPERTS_PER_STEP
    assert EXPERTS_PER_GROUP == 2 * MOE_EXPERTS_PER_STEP, "row classes assume two expert steps per group"
    compact_rows = tm + N_GROUPS * MOE_ROW_ALIGN
    buffer_rows = compact_rows + MOE_CHUNK
    row = pl.BlockSpec((tm, d), lambda i, j: (i, 0))
    return pl.pallas_call(
        _moe_kernel,
        grid=(t // tm, n_steps),
        in_specs=[row,
                  pl.BlockSpec((tm, ROUTER_LANES), lambda i, j: (i, 0)),
                  pl.BlockSpec((ROUTER_LANES, tm), lambda i, j: (0, i)),
                  pl.BlockSpec((MOE_EXPERTS_PER_STEP, d, 2 * D_EXPERT), lambda i, j: (j, 0, 0)),
                  pl.BlockSpec((MOE_EXPERTS_PER_STEP, D_EXPERT, d), lambda i, j: (j, 0, 0)),
                  row,
                  pl.BlockSpec((1, 6, d), lambda i, j: (i // tiles_per_batch, 0, 0)),
                  pl.BlockSpec((1, d), lambda i, j: (0, 0))],
        out_specs=row,
        out_shape=jax.ShapeDtypeStruct((t, d), F32),
        scratch_shapes=[pltpu.VMEM((compact_rows, tm), BF16),
                        pltpu.VMEM((buffer_rows, d), BF16),
                        pltpu.VMEM((buffer_rows, ROUTER_LANES), F32),
                        pltpu.VMEM((buffer_rows, d), F32),
                        pltpu.SMEM((2 * n_steps,), jnp.int32)],
        compiler_params=_params(("arbitrary", "arbitrary")),
        name="moe",
    )(h2, comb, combt, wgu, wd, x1, mod3, final_g.reshape(1, d))


def _router_weights(w_rg, b_rg, w_re, b_re):
    d = w_rg.shape[0]
    pad = ROUTER_LANES - N_EXPERTS - N_GROUPS
    w = jnp.concatenate([w_re, w_rg, jnp.zeros((d, pad), F32)], axis=1)
    b = jnp.concatenate([b_re, b_rg, jnp.zeros((pad,), F32)]).reshape(1, ROUTER_LANES)
    w_hi = w.astype(BF16)
    w_lo = (w - w_hi.astype(F32)).astype(BF16)
    return jnp.concatenate([w_hi, w_lo], axis=1), b


def kernel(x, c, w_ada, b_ada, norm1_g, w_in, lambda_q1, lambda_k1, lambda_q2, lambda_k2, subln_g, w_attn_proj, sgu_ln_g, sgu_ln_b, sgu_w_s, sgu_b_s, w_sgu_proj, w_out, norm2_g, w_router_group, b_router_group, w_router_expert, b_router_expert, w_expert_gate_up, w_expert_down, final_g):
    batch, seq, d = x.shape
    assert w_ada.shape[0] == 1, "single-layer trunk"
    x2d = x.reshape(batch * seq, d)

    mod, lam = _ada(c, w_ada[0], b_ada[0], lambda_q1[0], lambda_k1[0], lambda_q2[0], lambda_k2[0])
    mod3 = mod.reshape(batch, 6, d)

    proj_a, proj_b, gate_sgu = _in_proj(x2d, mod3, norm1_g[0], w_in[0].astype(BF16), seq)
    attn = _attention(proj_a, lam, subln_g[0], batch, seq)

    bs_full = jnp.broadcast_to(sgu_b_s[0][:, :, None], (N_SGU_GROUPS, SGU_CHUNK, SGU_CHUNK))
    wr, br = _router_weights(w_router_group[0], b_router_group[0], w_router_expert[0], b_router_expert[0])
    x1, h2, logits = _mix(attn, proj_b, gate_sgu, x2d, mod3, w_attn_proj[0].astype(BF16),
                          sgu_ln_g[0].reshape(1, d), sgu_ln_b[0].reshape(1, d),
                          sgu_w_s[0].astype(BF16), bs_full, w_sgu_proj[0].astype(BF16),
                          w_out[0].astype(BF16), norm2_g[0].reshape(1, d), wr, br, seq)
    comb, combt = _route(logits)
    out = _moe(h2, comb, combt, w_expert_gate_up[0].astype(BF16), w_expert_down[0].astype(BF16),
               x1, mod3, final_g, seq)
    return out.reshape(batch, seq, d)
```

```python
import functools
import math

import jax
import jax.numpy as jnp
import numpy as np
from jax import lax
from jax.experimental import pallas as pl
from jax.experimental.pallas import tpu as pltpu

D_MODEL = 1024
N_HEADS = 8
HEAD_DIM = 64
HEAD_WIDTH = 2 * HEAD_DIM
N_SGU_GROUPS = 8
SGU_CHUNK = 128
N_GROUPS = 4
EXPERTS_PER_GROUP = 8
N_EXPERTS = N_GROUPS * EXPERTS_PER_GROUP
D_EXPERT = 256
N_SEGMENTS = 7
EPS = 1e-6
LAMBDA_INIT = 0.8 - 0.6 * math.exp(-0.3 * 0)
ALIBI_SLOPES = np.array([2.0 ** (-8.0 * (h + 1) / N_HEADS) for h in range(N_HEADS)], dtype=np.float32)
ROUTER_LANES = 128
ROUTE_TILE = 4096
ROUTER_ROWS = 40
GROUP_ID_LANE = 64
MOE_TILE = 1024
MOE_CHUNK = 128
MOE_CLASSES = 3
MOE_ROW_ALIGN = 16
MOE_EXPERTS_PER_STEP = 4
LOG2_E = math.log2(math.e)
IN_PROJ_TILE = 1024
IN_PROJ_STEPS = 3
IN_PROJ_SUB = 256
MIX_TILE = 512
ATTN_ROWS = 256
ATTN_KEY_TILE = 1024
ATTN_VT_ROWS = HEAD_WIDTH + 16
ATTN_L_MIN = 2.0 ** -60
ATTN_L_MAX = 2.0 ** 100

F32 = jnp.float32
BF16 = jnp.bfloat16
VMEM_LIMIT_BYTES = 56 * 1024 * 1024


def _params(semantics):
    return pltpu.CompilerParams(dimension_semantics=semantics, vmem_limit_bytes=VMEM_LIMIT_BYTES)


def _dot(a, b):
    return jnp.dot(a, b, preferred_element_type=F32)


def _sigmoid(x):
    return 1.0 / (1.0 + jnp.exp(-x))


def _gelu_tanh(x):
    return 0.5 * x * (1.0 + jnp.tanh(math.sqrt(2.0 / math.pi) * (x + 0.044715 * (x * x * x))))


def _ada_kernel(c_ref, w_ref, b_ref, lq1_ref, lk1_ref, lq2_ref, lk2_ref, mod_ref, lam_ref):
    c = c_ref[...]
    act = c * _sigmoid(c)
    mod_ref[...] = jnp.dot(act, w_ref[...], preferred_element_type=F32,
                           precision=lax.Precision.HIGHEST) + b_ref[...]
    d1 = jnp.sum(lq1_ref[...] * lk1_ref[...], axis=-1, keepdims=True)
    d2 = jnp.sum(lq2_ref[...] * lk2_ref[...], axis=-1, keepdims=True)
    lam_ref[...] = jnp.exp(d1) - jnp.exp(d2) + LAMBDA_INIT


def _ada(c, w_ada, b_ada, lq1, lk1, lq2, lk2):
    batch, d = c.shape
    n = w_ada.shape[1]
    tn = 1024
    vec = pl.BlockSpec((1, HEAD_DIM), lambda j: (0, 0))
    return pl.pallas_call(
        _ada_kernel,
        grid=(n // tn,),
        in_specs=[pl.BlockSpec((batch, d), lambda j: (0, 0)),
                  pl.BlockSpec((d, tn), lambda j: (0, j)),
                  pl.BlockSpec((1, tn), lambda j: (0, j)),
                  vec, vec, vec, vec],
        out_specs=[pl.BlockSpec((batch, tn), lambda j: (0, j)),
                   pl.BlockSpec((1, 1), lambda j: (0, 0))],
        out_shape=[jax.ShapeDtypeStruct((batch, n), F32), jax.ShapeDtypeStruct((1, 1), F32)],
        compiler_params=_params(("arbitrary",)),
        name="ada",
    )(c, w_ada, b_ada.reshape(1, n), lq1.reshape(1, -1), lk1.reshape(1, -1),
      lq2.reshape(1, -1), lk2.reshape(1, -1))


def _in_proj_kernel(x_ref, mod_ref, g_ref, w_ref, oa_ref, ob_ref, oc_ref, h_scr):
    d = h_scr.shape[1]
    x = x_ref[...]
    y = x * lax.rsqrt(jnp.mean(x * x, axis=-1, keepdims=True) + EPS) * g_ref[...]
    h_scr[...] = (y * (1.0 + mod_ref[0, 1:2, :]) + mod_ref[0, 0:1, :]).astype(BF16)

    def w(segment, cs):
        return w_ref[:, segment * d + cs.start:segment * d + cs.stop]

    def project(step, epilogue_a, epilogue_b, gate_subtiles):
        for n in range(d // IN_PROJ_SUB):
            cs = slice(n * IN_PROJ_SUB, (n + 1) * IN_PROJ_SUB)
            oa_ref[step, :, cs] = epilogue_a(_dot(h_scr[...], w(step, cs))).astype(BF16)
            ob_ref[step, :, cs] = epilogue_b(_dot(h_scr[...], w(IN_PROJ_STEPS + step, cs))).astype(BF16)
            if n < len(gate_subtiles):
                gs = slice(gate_subtiles[n] * IN_PROJ_SUB, (gate_subtiles[n] + 1) * IN_PROJ_SUB)
                oc_ref[:, gs] = _sigmoid(_dot(h_scr[...], w(N_SEGMENTS - 1, gs))).astype(BF16)

    project(0, lambda acc: acc * (HEAD_DIM ** -0.5 * LOG2_E), _gelu_tanh, (0, 1))
    project(1, lambda acc: acc, _gelu_tanh, (2,))
    project(2, lambda acc: acc, _sigmoid, (3,))


def _in_proj(x2d, mod3, norm_g, w_in_bf, seq):
    t, d = x2d.shape
    tm = IN_PROJ_TILE
    tiles_per_batch = seq // tm
    assert d // IN_PROJ_SUB == 4 and N_SEGMENTS == 2 * IN_PROJ_STEPS + 1
    stack = pl.BlockSpec((IN_PROJ_STEPS, tm, d), lambda i: (0, i, 0))
    return pl.pallas_call(
        _in_proj_kernel,
        grid=(t // tm,),
        in_specs=[pl.BlockSpec((tm, d), lambda i: (i, 0)),
                  pl.BlockSpec((1, 6, d), lambda i: (i // tiles_per_batch, 0, 0)),
                  pl.BlockSpec((1, d), lambda i: (0, 0)),
                  pl.BlockSpec((d, N_SEGMENTS * d), lambda i: (0, 0), pipeline_mode=pl.Buffered(1))],
        out_specs=[stack, stack, pl.BlockSpec((tm, d), lambda i: (i, 0))],
        out_shape=[jax.ShapeDtypeStruct((IN_PROJ_STEPS, t, d), BF16), jax.ShapeDtypeStruct((IN_PROJ_STEPS, t, d), BF16),
                   jax.ShapeDtypeStruct((t, d), BF16)],
        scratch_shapes=[pltpu.VMEM((tm, d), BF16)],
        compiler_params=_params(("arbitrary",)),
        name="in_proj",
    )(x2d, mod3, norm_g.reshape(1, d), w_in_bf)


def _attn_kernel(slopes_ref, lam_ref, q_ref, k_ref, v_ref, subg_ref, subg_col_ref, o_ref,
                 bias_scr, p_scr, vt_scr, *, rows, seq):
    h = pl.program_id(0)
    b = pl.program_id(1)
    nq = seq // rows
    contract_last = (((1,), (1,)), ((), ()))

    @pl.when(b == 0)
    def _():
        slope = -LOG2_E * slopes_ref[h]
        for qb in range(nq):
            kpos = lax.broadcasted_iota(jnp.int32, (seq, rows), 0)
            qpos = qb * rows + lax.broadcasted_iota(jnp.int32, (seq, rows), 1)
            bias_scr[qb] = slope * jnp.abs(qpos - kpos).astype(F32)
        sub = lax.broadcasted_iota(jnp.int32, (ATTN_VT_ROWS - HEAD_WIDTH, seq), 0)
        vt_scr[HEAD_WIDTH:, :] = jnp.where(sub == 0, 1.0, 0.0).astype(BF16)

    vt_scr[:HEAD_WIDTH, :] = v_ref[0].astype(F32).T.astype(BF16)
    lam = lam_ref[0, 0]

    def masked_q(r0):
        q = q_ref[0, pl.ds(r0, rows), :]
        lane = lax.broadcasted_iota(jnp.int32, q.shape, 1)
        zero = jnp.zeros_like(q)
        return jnp.where(lane < HEAD_DIM, q, zero), jnp.where(lane >= HEAD_DIM, q, zero)

    def fast_block(qb, n_bad):
        r0 = pl.multiple_of(qb * rows, rows)
        q0, q1 = masked_q(r0)
        qq = jnp.concatenate([q0, q1], axis=0)
        for c in range(seq // ATTN_KEY_TILE):
            ks = slice(c * ATTN_KEY_TILE, (c + 1) * ATTN_KEY_TILE)
            st = lax.dot_general(k_ref[0, ks, :], qq, contract_last, preferred_element_type=F32)
            bias = bias_scr[qb, ks, :]
            p_scr[ks, :] = jnp.exp2(st + jnp.concatenate([bias, bias], axis=1)).astype(BF16)
        oo = _dot(vt_scr[...], p_scr[...])
        l0 = oo[HEAD_WIDTH:HEAD_WIDTH + 1, :rows]
        l1 = oo[HEAD_WIDTH:HEAD_WIDTH + 1, rows:]
        ot = oo[:HEAD_WIDTH, :rows] / l0 - lam * (oo[:HEAD_WIDTH, rows:] / l1)
        ot = ot * lax.rsqrt(jnp.mean(ot * ot, axis=0, keepdims=True) + EPS)
        ot = ot * subg_col_ref[...] * (1.0 - LAMBDA_INIT)
        o_ref[pl.ds(r0, rows), :] = ot.T.astype(BF16)
        ok = ((l0 >= ATTN_L_MIN) & (l0 <= ATTN_L_MAX)) & ((l1 >= ATTN_L_MIN) & (l1 <= ATTN_L_MAX))
        return n_bad + jnp.where(ok, 0.0, 1.0)

    n_bad = lax.fori_loop(0, nq, fast_block, jnp.zeros((1, rows), F32), unroll=True)

    @pl.when(jnp.sum(n_bad) > 0.0)
    def _():
        subg = subg_ref[...]

        def safe_block(qb, carry):
            r0 = pl.multiple_of(qb * rows, rows)
            outs = []
            for qm in masked_q(r0):
                s = lax.dot_general(qm, k_ref[0], contract_last, preferred_element_type=F32) + bias_scr[qb].T
                p = jnp.exp2(s - jnp.max(s, axis=-1, keepdims=True))
                outs.append(_dot(p.astype(BF16), v_ref[0]) / jnp.sum(p, axis=-1, keepdims=True))
            o = outs[0] - lam * outs[1]
            o = o * lax.rsqrt(jnp.mean(o * o, axis=-1, keepdims=True) + EPS)
            o_ref[pl.ds(r0, rows), :] = (o * subg * (1.0 - LAMBDA_INIT)).astype(BF16)
            return carry

        lax.fori_loop(0, nq, safe_block, 0)


def _attention(qkv, lam, subln_g, batch, seq):
    t = batch * seq
    rows = ATTN_ROWS
    kernel = functools.partial(_attn_kernel, rows=rows, seq=seq)
    smem = pl.BlockSpec(memory_space=pltpu.SMEM)
    seg = lambda k: pl.BlockSpec((1, seq, HEAD_WIDTH), lambda h, b, k=k: (k, b, h))
    return pl.pallas_call(
        kernel,
        grid=(N_HEADS, batch),
        in_specs=[smem, smem, seg(0), seg(1), seg(2),
                  pl.BlockSpec((1, HEAD_WIDTH), lambda h, b: (0, 0)),
                  pl.BlockSpec((HEAD_WIDTH, 1), lambda h, b: (0, 0))],
        out_specs=pl.BlockSpec((seq, HEAD_WIDTH), lambda h, b: (b, h)),
        out_shape=jax.ShapeDtypeStruct((t, N_HEADS * HEAD_WIDTH), BF16),
        scratch_shapes=[pltpu.VMEM((seq // rows, seq, rows), F32),
                        pltpu.VMEM((seq, 2 * rows), BF16),
                        pltpu.VMEM((ATTN_VT_ROWS, seq), BF16)],
        compiler_params=_params(("arbitrary", "arbitrary")),
        name="diff_attn",
    )(jnp.asarray(ALIBI_SLOPES), lam, qkv, qkv, qkv, subln_g.reshape(1, HEAD_WIDTH),
      subln_g.reshape(HEAD_WIDTH, 1))


def _mix_kernel(attn_ref, u_ref, s_ref, ga_ref, gb_ref, x_ref, mod_ref,
                wap_ref, lng_ref, lnb_ref, ws_ref, bs_ref, wsp_ref, wout_ref,
                n2g_ref, wr_ref, br_ref,
                x1_ref, h2_ref, logit_ref, gated_scr):
    tm = x_ref.shape[0]
    y_attn = _dot(attn_ref[...], wap_ref[...])

    s = s_ref[0].astype(F32)
    mu = jnp.mean(s, axis=-1, keepdims=True)
    sc = s - mu
    var = jnp.mean(sc * sc, axis=-1, keepdims=True)
    v = ((sc * lax.rsqrt(var + EPS)) * lng_ref[...] + lnb_ref[...]).astype(BF16)
    n_chunks = tm // SGU_CHUNK
    for g in range(N_SGU_GROUPS):
        cols = slice(g * SGU_CHUNK, (g + 1) * SGU_CHUNK)
        v_g = jnp.concatenate([v[c * SGU_CHUNK:(c + 1) * SGU_CHUNK, cols] for c in range(n_chunks)], axis=1)
        mixed_g = _dot(ws_ref[g], v_g)
        for c in range(n_chunks):
            rows = slice(c * SGU_CHUNK, (c + 1) * SGU_CHUNK)
            mixed = mixed_g[:, c * SGU_CHUNK:(c + 1) * SGU_CHUNK] + bs_ref[g]
            gated_scr[rows, cols] = (u_ref[0, rows, cols].astype(F32) * mixed).astype(BF16)
    y_sgu = _dot(gated_scr[...], wsp_ref[...])

    y = ga_ref[0].astype(F32) * y_attn + gb_ref[...].astype(F32) * y_sgu
    x1 = x_ref[...] + mod_ref[0, 2:3, :] * _dot(y.astype(BF16), wout_ref[...])
    x1_ref[...] = x1

    h2 = x1 * lax.rsqrt(jnp.mean(x1 * x1, axis=-1, keepdims=True) + EPS) * n2g_ref[...]
    h2 = h2 * (1.0 + mod_ref[0, 4:5, :]) + mod_ref[0, 3:4, :]
    h2_hi = h2.astype(BF16)
    h2_ref[...] = h2_hi
    h2_lo = (h2 - h2_hi.astype(F32)).astype(BF16)
    parts = _dot(jnp.concatenate([h2_hi, h2_lo], axis=0), wr_ref[...])
    logits = ((parts[:tm, :ROUTER_LANES] + parts[:tm, ROUTER_LANES:])
              + (parts[tm:, :ROUTER_LANES] + parts[tm:, ROUTER_LANES:]) + br_ref[...])
    logit_ref[...] = logits.T


def _mix(attn, proj_b, gate_sgu, x2d, mod3, wap, lng, lnb, ws, bs_full, wsp, wout, n2g, wr, br, seq):
    t, d = x2d.shape
    tm = MIX_TILE
    tiles_per_batch = seq // tm
    const2 = lambda shape: pl.BlockSpec(shape, lambda i: (0, 0))
    const3 = lambda shape: pl.BlockSpec(shape, lambda i: (0, 0, 0))
    seg = lambda k: pl.BlockSpec((1, tm, d), lambda i, k=k: (k, i, 0))
    row = pl.BlockSpec((tm, d), lambda i: (i, 0))
    return pl.pallas_call(
        _mix_kernel,
        grid=(t // tm,),
        in_specs=[row, seg(0), seg(1), seg(2), row, row,
                  pl.BlockSpec((1, 6, d), lambda i: (i // tiles_per_batch, 0, 0)),
                  const2((d, d)), const2((1, d)), const2((1, d)),
                  const3((N_SGU_GROUPS, SGU_CHUNK, SGU_CHUNK)),
                  const3((N_SGU_GROUPS, SGU_CHUNK, SGU_CHUNK)),
                  const2((d, d)), const2((d, d)), const2((1, d)),
                  const2((d, 2 * ROUTER_LANES)), const2((1, ROUTER_LANES))],
        out_specs=[row, row, pl.BlockSpec((ROUTER_LANES, tm), lambda i: (0, i))],
        out_shape=[jax.ShapeDtypeStruct((t, d), F32), jax.ShapeDtypeStruct((t, d), BF16),
                   jax.ShapeDtypeStruct((ROUTER_LANES, t), F32)],
        scratch_shapes=[pltpu.VMEM((tm, d), BF16)],
        compiler_params=_params(("arbitrary",)),
        name="mix",
    )(attn, proj_b, proj_b, proj_b, gate_sgu, x2d, mod3, wap, lng, lnb, ws, bs_full, wsp, wout, n2g, wr, br)


def _route_t(z):
    row = lax.broadcasted_iota(jnp.int32, z.shape, 0)
    neg = jnp.float32(-jnp.inf)
    big = jnp.int32(ROUTER_LANES)

    def first_argmax(val, vmax):
        return jnp.min(jnp.where(val == vmax, row, big), axis=0, keepdims=True)

    is_group = (row >= N_EXPERTS) & (row < N_EXPERTS + N_GROUPS)
    gl = jnp.where(is_group, z, neg)
    gmax = jnp.max(gl, axis=0, keepdims=True)
    ge = jnp.exp(gl - gmax)
    gp = ge / jnp.sum(ge, axis=0, keepdims=True)
    gval = jnp.max(gp, axis=0, keepdims=True)
    gidx = first_argmax(gp, gval) - N_EXPERTS

    in_group = (row >= gidx * EXPERTS_PER_GROUP) & (row < (gidx + 1) * EXPERTS_PER_GROUP)
    el = jnp.where(in_group, z, neg)
    emax = jnp.max(el, axis=0, keepdims=True)
    ee = jnp.exp(el - emax)
    ep = ee / jnp.sum(ee, axis=0, keepdims=True)
    ep = jnp.where(in_group, ep, -1.0)
    ev0 = jnp.max(ep, axis=0, keepdims=True)
    ei0 = first_argmax(ep, ev0)
    ep_rest = jnp.where(row == ei0, -1.0, ep)
    ev1 = jnp.max(ep_rest, axis=0, keepdims=True)
    ei1 = first_argmax(ep_rest, ev1)
    denom = ev0 + ev1
    weights = jnp.where(row == ei0, ev0 / denom * gval, jnp.where(row == ei1, ev1 / denom * gval, 0.0))
    tm = z.shape[1]
    combt = jnp.concatenate(
        [weights, jnp.zeros((GROUP_ID_LANE - ROUTER_ROWS, tm), F32),
         jnp.broadcast_to(gidx.astype(F32), (8, tm)) * (lax.broadcasted_iota(jnp.int32, (8, tm), 0) == 0),
         jnp.zeros((ROUTER_LANES - GROUP_ID_LANE - 8, tm), F32)], axis=0)
    return combt


def _route_kernel(logit_ref, comb_ref, combt_ref):
    combt = _route_t(logit_ref[:ROUTER_ROWS, :])
    combt_ref[...] = combt
    comb_ref[...] = combt.T


def _route(logits_t):
    t = logits_t.shape[1]
    tm = min(ROUTE_TILE, t)
    spec_t = pl.BlockSpec((ROUTER_LANES, tm), lambda i: (0, i))
    return pl.pallas_call(
        _route_kernel, grid=(t // tm,), in_specs=[spec_t],
        out_specs=[pl.BlockSpec((tm, ROUTER_LANES), lambda i: (i, 0)), spec_t],
        out_shape=[jax.ShapeDtypeStruct((t, ROUTER_LANES), F32), jax.ShapeDtypeStruct((ROUTER_LANES, t), F32)],
        compiler_params=_params(("arbitrary",)), name="route",
    )(logits_t)


def _moe_kernel(h2_ref, logit_ref, wgu_ref, wd_ref, x1_ref, mod_ref, fg_ref, out_ref,
                onehot_scr, xs_scr, wl_scr, outc_scr, meta_ref):
    j = pl.program_id(1)
    tt = h2_ref.shape[0]
    rc = onehot_scr.shape[0]

    @pl.when(j == 0)
    def _():
        ct = _route_t(logit_ref[:ROUTER_ROWS, :])
        gid = ct[GROUP_ID_LANE:GROUP_ID_LANE + 1, :]
        erow = lax.broadcasted_iota(jnp.int32, ct.shape, 0)
        first = (erow < N_EXPERTS) & ((erow & (EXPERTS_PER_GROUP - 1)) < MOE_EXPERTS_PER_STEP)
        second = (erow < N_EXPERTS) & ((erow & (EXPERTS_PER_GROUP - 1)) >= MOE_EXPERTS_PER_STEP)
        need_first = jnp.max(jnp.where(first, ct, 0.0), axis=0, keepdims=True) > 0.0
        need_second = jnp.max(jnp.where(second, ct, 0.0), axis=0, keepdims=True) > 0.0
        cls = jnp.where(need_first, jnp.where(need_second, 1.0, 0.0), 2.0)
        key = gid * MOE_CLASSES + cls
        sub = lax.broadcasted_iota(jnp.int32, (N_GROUPS * MOE_CLASSES + 4, tt), 0).astype(F32)
        member = jnp.where(key == sub, 1.0, 0.0)
        before = jnp.where(lax.broadcasted_iota(jnp.int32, (tt, tt), 0)
                           < lax.broadcasted_iota(jnp.int32, (tt, tt), 1), 1.0, 0.0).astype(BF16)
        rank = _dot(member.astype(BF16), before)
        pos = jnp.zeros((1, tt), F32)
        off = jnp.int32(0)
        n_steps = pl.num_programs(1)
        windows = lambda n_rows: (n_rows + (MOE_CHUNK - 1)) // MOE_CHUNK
        for gg in range(N_GROUPS):
            start = off
            counts = []
            for c in range(MOE_CLASSES):
                r = gg * MOE_CLASSES + c
                counts.append(jnp.sum(member[r:r + 1, :]).astype(jnp.int32))
                pos = pos + member[r:r + 1, :] * (start.astype(F32) + rank[r:r + 1, :])
                start = start + counts[c]
            second_start = (off + counts[0]) // MOE_ROW_ALIGN * MOE_ROW_ALIGN
            meta_ref[2 * gg] = off
            meta_ref[n_steps + 2 * gg] = windows(counts[0] + counts[1])
            meta_ref[2 * gg + 1] = second_start
            meta_ref[n_steps + 2 * gg + 1] = windows(start - second_start)
            off = (start + (MOE_ROW_ALIGN - 1)) // MOE_ROW_ALIGN * MOE_ROW_ALIGN
        row = lax.broadcasted_iota(jnp.int32, (rc, tt), 0).astype(F32)
        onehot = jnp.where(row == pos, 1.0, 0.0).astype(BF16)
        onehot_scr[...] = onehot
        xs_scr[:rc, :] = _dot(onehot, h2_ref[...]).astype(BF16)
        xs_scr[rc:, :] = jnp.zeros((xs_scr.shape[0] - rc, xs_scr.shape[1]), BF16)
        c_hi = ct.astype(BF16)
        c_lo = (ct - c_hi.astype(F32)).astype(BF16)
        wl2 = lax.dot_general(onehot, jnp.concatenate([c_hi, c_lo], axis=0), (((1,), (1,)), ((), ())),
                              preferred_element_type=F32)
        wl_scr[:rc, :] = wl2[:, :ROUTER_LANES] + wl2[:, ROUTER_LANES:]
        wl_scr[rc:, :] = jnp.zeros((wl_scr.shape[0] - rc, ROUTER_LANES), F32)
        outc_scr[...] = jnp.zeros_like(outc_scr)

    off_g = meta_ref[j]
    n_chunks_g = meta_ref[pl.num_programs(1) + j]

    def experts(r0, rows):
        xs = xs_scr[pl.ds(r0, rows), :]
        wl = wl_scr[pl.ds(r0, rows), :]
        lane = lax.broadcasted_iota(jnp.int32, wl.shape, 1)
        acts = []
        for e in range(MOE_EXPERTS_PER_STEP):
            gu = _dot(xs, wgu_ref[e])
            gate = gu[:, :D_EXPERT]
            up = gu[:, D_EXPERT:]
            w = jnp.sum(jnp.where(lane == j * MOE_EXPERTS_PER_STEP + e, wl, 0.0), axis=-1, keepdims=True)
            acts.append((gate * _sigmoid(gate) * up * w).astype(BF16))
        wd_all = wd_ref[...].reshape(MOE_EXPERTS_PER_STEP * D_EXPERT, D_MODEL)
        outc_scr[pl.ds(r0, rows), :] += _dot(jnp.concatenate(acts, axis=1), wd_all)

    def chunk_pair(k, carry):
        experts(pl.multiple_of(off_g + k * (2 * MOE_CHUNK), MOE_ROW_ALIGN), 2 * MOE_CHUNK)
        return carry

    lax.fori_loop(0, n_chunks_g // 2, chunk_pair, 0)

    @pl.when(n_chunks_g % 2 == 1)
    def _():
        experts(pl.multiple_of(off_g + (n_chunks_g - 1) * MOE_CHUNK, MOE_ROW_ALIGN), MOE_CHUNK)

    @pl.when(j == pl.num_programs(1) - 1)
    def _():
        y = lax.dot_general(onehot_scr[...], outc_scr[:rc, :].astype(BF16), (((0,), (0,)), ((), ())),
                            preferred_element_type=F32)
        x2 = x1_ref[...] + mod_ref[0, 5:6, :] * y
        out_ref[...] = x2 * lax.rsqrt(jnp.mean(x2 * x2, axis=-1, keepdims=True) + EPS) * fg_ref[...]


def _moe(h2, logits_t, wgu, wd, x1, mod3, final_g, seq):
    t, d = x1.shape
    tm = MOE_TILE
    tiles_per_batch = seq // tm
    n_steps = N_EXPERTS // MOE_EXPERTS_PER_STEP
    assert EXPERTS_PER_GROUP == 2 * MOE_EXPERTS_PER_STEP, "row classes assume two expert steps per group"
    compact_rows = tm + N_GROUPS * MOE_ROW_ALIGN
    buffer_rows = compact_rows + MOE_CHUNK
    row = pl.BlockSpec((tm, d), lambda i, j: (i, 0))
    return pl.pallas_call(
        _moe_kernel,
        grid=(t // tm, n_steps),
        in_specs=[row,
                  pl.BlockSpec((ROUTER_LANES, tm), lambda i, j: (0, i)),
                  pl.BlockSpec((MOE_EXPERTS_PER_STEP, d, 2 * D_EXPERT), lambda i, j: (j, 0, 0)),
                  pl.BlockSpec((MOE_EXPERTS_PER_STEP, D_EXPERT, d), lambda i, j: (j, 0, 0)),
                  row,
                  pl.BlockSpec((1, 6, d), lambda i, j: (i // tiles_per_batch, 0, 0)),
                  pl.BlockSpec((1, d), lambda i, j: (0, 0))],
        out_specs=row,
        out_shape=jax.ShapeDtypeStruct((t, d), F32),
        scratch_shapes=[pltpu.VMEM((compact_rows, tm), BF16),
                        pltpu.VMEM((buffer_rows, d), BF16),
                        pltpu.VMEM((buffer_rows, ROUTER_LANES), F32),
                        pltpu.VMEM((buffer_rows, d), F32),
                        pltpu.SMEM((2 * n_steps,), jnp.int32)],
        compiler_params=_params(("arbitrary", "arbitrary")),
        name="moe",
    )(h2, logits_t, wgu, wd, x1, mod3, final_g.reshape(1, d))


def _router_weights(w_rg, b_rg, w_re, b_re):
    d = w_rg.shape[0]
    pad = ROUTER_LANES - N_EXPERTS - N_GROUPS
    w = jnp.concatenate([w_re, w_rg, jnp.zeros((d, pad), F32)], axis=1)
    b = jnp.concatenate([b_re, b_rg, jnp.zeros((pad,), F32)]).reshape(1, ROUTER_LANES)
    w_hi = w.astype(BF16)
    w_lo = (w - w_hi.astype(F32)).astype(BF16)
    return jnp.concatenate([w_hi, w_lo], axis=1), b


def kernel(x, c, w_ada, b_ada, norm1_g, w_in, lambda_q1, lambda_k1, lambda_q2, lambda_k2, subln_g, w_attn_proj, sgu_ln_g, sgu_ln_b, sgu_w_s, sgu_b_s, w_sgu_proj, w_out, norm2_g, w_router_group, b_router_group, w_router_expert, b_router_expert, w_expert_gate_up, w_expert_down, final_g):
    batch, seq, d = x.shape
    assert w_ada.shape[0] == 1, "single-layer trunk"
    x2d = x.reshape(batch * seq, d)

    mod, lam = _ada(c, w_ada[0], b_ada[0], lambda_q1[0], lambda_k1[0], lambda_q2[0], lambda_k2[0])
    mod3 = mod.reshape(batch, 6, d)

    proj_a, proj_b, gate_sgu = _in_proj(x2d, mod3, norm1_g[0], w_in[0].astype(BF16), seq)
    attn = _attention(proj_a, lam, subln_g[0], batch, seq)

    bs_full = jnp.broadcast_to(sgu_b_s[0][:, :, None], (N_SGU_GROUPS, SGU_CHUNK, SGU_CHUNK))
    wr, br = _router_weights(w_router_group[0], b_router_group[0], w_router_expert[0], b_router_expert[0])
    x1, h2, logits = _mix(attn, proj_b, gate_sgu, x2d, mod3, w_attn_proj[0].astype(BF16),
                          sgu_ln_g[0].reshape(1, d), sgu_ln_b[0].reshape(1, d),
                          sgu_w_s[0].astype(BF16), bs_full, w_sgu_proj[0].astype(BF16),
                          w_out[0].astype(BF16), norm2_g[0].reshape(1, d), wr, br, seq)
    out = _moe(h2, logits,w_expert_gate_up[0].astype(BF16), w_expert_down[0].astype(BF16),
               x1, mod3, final_g, seq)
    return out.reshape(batch, seq, d)
```
